```python
import jax, jax.numpy as jnp
from jax import lax
import numpy as np

D_MODEL = 1024
BATCH = 8
SEQ = 4096
DEPTH = 1

RW_HEAD_DIM = 64
RW_WIDTH = D_MODEL // 2
RW_HEADS = RW_WIDTH // RW_HEAD_DIM
RW_DECAY_LORA = 64
RW_AAA_LORA = 64
RW_GATE_LORA = 128
RW_COLS = 3 * RW_WIDTH + RW_DECAY_LORA + RW_AAA_LORA + RW_GATE_LORA
RW_LNX_EPS = 64e-5
GM_WIDTH = D_MODEL // 2
GM_GROUP_DIM = 64
GM_GROUPS = GM_WIDTH // GM_GROUP_DIM
GM_CHUNK = 128
N_BRANCH = 2
IN_COLS = RW_COLS + 2 * GM_WIDTH + N_BRANCH * D_MODEL
N_EXPERTS = 256
TOP_K = 8
EXPERT_DIM = D_MODEL // 4
SHARED_DIM = D_MODEL // 4
ROUTED_SCALE = 2.5
MOE_BLOCK = 128
NORM_EPS = 1e-6
LN_EPS = 1e-5

kernel_name = 'hybrid_rwkv7_gmlp_moe_adaln'


def rmsnorm(x, g):
    xf = x.astype(jnp.float32)
    y = xf * lax.rsqrt(jnp.mean(xf * xf, -1, keepdims=True) + NORM_EPS)
    return (y * g.astype(jnp.float32)).astype(x.dtype)


def token_shift(p, mu):
    prev = jnp.pad(p, ((0, 0), (1, 0), (0, 0)))[:, :-1]
    return p + (prev - p) * mu


def wkv7_scan(r, w, k, v, a, b):
    Bb, Ss, H, N = r.shape

    def step(st, inp):
        r_t, w_t, k_t, v_t, a_t, b_t = inp
        sa = jnp.einsum('bhij,bhj->bhi', st, a_t)
        st = st * w_t[:, :, None, :] + sa[..., None] * b_t[:, :, None, :] + v_t[..., None] * k_t[:, :, None, :]
        return st, jnp.einsum('bhij,bhj->bhi', st, r_t)

    xs = tuple(jnp.moveaxis(t, 1, 0) for t in (r, w, k, v, a, b))
    _, ys = lax.scan(step, jnp.zeros((Bb, H, N, N), jnp.float32), xs)
    return jnp.moveaxis(ys, 0, 1)


def rwkv7_branch(p_rw, mu, w0, w2, a0, a2, g2, k_k, k_a, r_k, lnx_g, lnx_b):
    f32 = jnp.float32
    Bb, Ss, _ = p_rw.shape
    C = RW_WIDTH
    xs = token_shift(p_rw.astype(f32), mu)
    r = xs[..., :C]
    k = xs[..., C:2 * C]
    v = xs[..., 2 * C:3 * C]
    o = 3 * C
    xw = xs[..., o:o + RW_DECAY_LORA]
    o += RW_DECAY_LORA
    xa = xs[..., o:o + RW_AAA_LORA]
    o += RW_AAA_LORA
    xg = xs[..., o:o + RW_GATE_LORA]
    w_log = -jax.nn.softplus(-(w0 + jnp.tanh(xw) @ w2)) - 0.5
    decay = jnp.exp(-jnp.exp(w_log))
    a = jax.nn.sigmoid(a0 + xa @ a2)
    g = jax.nn.sigmoid(xg) @ g2

    def heads(t):
        return t.reshape(Bb, Ss, RW_HEADS, RW_HEAD_DIM)

    kk = heads(k * k_k)
    kk = kk * lax.rsqrt(jnp.maximum(jnp.sum(kk * kk, -1, keepdims=True), 1e-24))
    k = k * (1.0 + (a - 1.0) * k_a)
    rh, kh, vh, ah = heads(r), heads(k), heads(v), heads(a)
    y = wkv7_scan(rh, heads(decay), kh, vh, -kk, kk * ah)
    m = jnp.mean(y, -1, keepdims=True)
    var = jnp.mean(jnp.square(y - m), -1, keepdims=True)
    y = (y - m) * lax.rsqrt(var + RW_LNX_EPS) * lnx_g.reshape(RW_HEADS, RW_HEAD_DIM) \
        + lnx_b.reshape(RW_HEADS, RW_HEAD_DIM)
    y = y + jnp.sum(rh * kh * r_k, -1, keepdims=True) * vh
    return y.reshape(Bb, Ss, C) * g


def gmlp_branch(p_gm, ln_g, ln_b, w_s, b_s):
    f32 = jnp.float32
    Bb, Ss, _ = p_gm.shape
    nC = Ss // GM_CHUNK
    z = jax.nn.gelu(p_gm.astype(f32))
    u = z[..., :GM_WIDTH].reshape(Bb, nC, GM_CHUNK, GM_GROUPS, GM_GROUP_DIM)
    v = z[..., GM_WIDTH:].reshape(Bb, nC, GM_CHUNK, GM_GROUPS, GM_GROUP_DIM)
    m = jnp.mean(v, -1, keepdims=True)
    var = jnp.mean(jnp.square(v - m), -1, keepdims=True)
    vn = (v - m) * lax.rsqrt(var + LN_EPS) * ln_g + ln_b
    ws = w_s * jnp.tril(jnp.ones((GM_CHUNK, GM_CHUNK), f32))
    sv = jnp.einsum('gts,bnsgc->bntgc', ws, vn) + jnp.swapaxes(b_s, 0, 1)[:, :, None]
    return (u * sv).reshape(Bb, Ss, GM_WIDTH)


def swiglu(x, wg, wu, wd):
    return (jax.nn.silu(x @ wg) * (x @ wu)) @ wd


def moe(h, router_w, router_b, w_gate, w_up, w_down, s_gate, s_up, s_down):
    f32 = jnp.float32
    Bb, Ss, D = h.shape
    N = Bb * Ss
    NK = N * TOP_K
    hf = h.reshape(N, D)
    scores = jax.nn.sigmoid((hf @ router_w).astype(f32))
    _, idx = lax.top_k(scores + router_b.astype(f32), TOP_K)
    sel = jnp.take_along_axis(scores, idx, axis=1)
    wts = sel / jnp.sum(sel, -1, keepdims=True) * ROUTED_SCALE
    flat_e = idx.reshape(-1)
    flat_tok = jnp.arange(NK, dtype=jnp.int32) // TOP_K
    flat_w = wts.reshape(-1)
    order = jnp.argsort(flat_e)
    se = flat_e[order]
    counts = jnp.bincount(flat_e, length=N_EXPERTS)
    starts = jnp.cumsum(counts) - counts
    padded = (counts + MOE_BLOCK - 1) // MOE_BLOCK * MOE_BLOCK
    pends = jnp.cumsum(padded)
    pstarts = pends - padded
    dest = pstarts[se] + (jnp.arange(NK, dtype=jnp.int32) - starts[se])
    P = NK + N_EXPERTS * MOE_BLOCK
    nb = P // MOE_BLOCK
    row_tok = jnp.full((P,), N, jnp.int32).at[dest].set(flat_tok[order])
    row_w = jnp.zeros((P,), f32).at[dest].set(flat_w[order])
    block_e = jnp.minimum(jnp.searchsorted(pends, jnp.arange(nb, dtype=jnp.int32) * MOE_BLOCK, side='right'), N_EXPERTS - 1)
    hpad = jnp.concatenate([hf, jnp.zeros((1, D), hf.dtype)], axis=0)

    def body(y, blk):
        tok, wt, e = blk
        xb = hpad[tok]
        ob = swiglu(xb, w_gate[e], w_up[e], w_down[e]) * wt[:, None].astype(xb.dtype)
        return y.at[tok].add(ob.astype(y.dtype)), None

    y, _ = lax.scan(body, jnp.zeros((N + 1, D), hf.dtype),
                    (row_tok.reshape(nb, MOE_BLOCK), row_w.reshape(nb, MOE_BLOCK), block_e))
    out = y[:N] + swiglu(hf, s_gate, s_up, s_down)
    return out.reshape(Bb, Ss, D)


def setup_inputs(seed: int = 0) -> dict:
    key = jax.random.key(seed)
    ks = iter(jax.random.split(key, 48))
    f32 = jnp.float32
    L = DEPTH
    D = D_MODEL

    def nrm(shape, scale):
        return jax.random.normal(next(ks), shape, f32) * scale

    def near_one(shape):
        return 1.0 + 0.1 * jax.random.normal(next(ks), shape, f32)

    return {
        'x': nrm((BATCH, SEQ, D), 1.0),
        'c': nrm((BATCH, D), 1.0),
        'w_ada': nrm((L, D, 6 * D), 0.5 * D ** -0.5),
        'b_ada': nrm((L, 6 * D), 0.02),
        'norm1_g': near_one((L, D)),
        'w_in': nrm((L, D, IN_COLS), D ** -0.5),
        'rw_mu': jax.random.uniform(next(ks), (L, RW_COLS), f32),
        'rw_w0': nrm((L, RW_WIDTH), 0.5) - 0.5,
        'rw_w2': nrm((L, RW_DECAY_LORA, RW_WIDTH), 0.5 * RW_DECAY_LORA ** -0.5),
        'rw_a0': nrm((L, RW_WIDTH), 0.5),
        'rw_a2': nrm((L, RW_AAA_LORA, RW_WIDTH), 0.5 * RW_AAA_LORA ** -0.5),
        'rw_g2': nrm((L, RW_GATE_LORA, RW_WIDTH), RW_GATE_LORA ** -0.5),
        'rw_k_k': near_one((L, RW_WIDTH)),
        'rw_k_a': near_one((L, RW_WIDTH)),
        'rw_r_k': nrm((L, RW_HEADS, RW_HEAD_DIM), 0.1),
        'rw_lnx_g': near_one((L, RW_WIDTH)),
        'rw_lnx_b': nrm((L, RW_WIDTH), 0.02),
        'gm_ln_g': near_one((L, GM_GROUPS, GM_GROUP_DIM)),
        'gm_ln_b': nrm((L, GM_GROUPS, GM_GROUP_DIM), 0.02),
        'gm_w_s': nrm((L, GM_GROUPS, GM_CHUNK, GM_CHUNK), GM_CHUNK ** -0.5),
        'gm_b_s': near_one((L, GM_GROUPS, GM_CHUNK)),
        'w_br_rwkv': nrm((L, RW_WIDTH, D), RW_WIDTH ** -0.5),
        'w_br_gmlp': nrm((L, GM_WIDTH, D), GM_WIDTH ** -0.5),
        'w_out': nrm((L, D, D), D ** -0.5),
        'norm2_g': near_one((L, D)),
        'router_w': nrm((L, D, N_EXPERTS), D ** -0.5),
        'router_b': nrm((L, N_EXPERTS), 0.01),
        'moe_w_gate': nrm((L, N_EXPERTS, D, EXPERT_DIM), D ** -0.5),
        'moe_w_up': nrm((L, N_EXPERTS, D, EXPERT_DIM), D ** -0.5),
        'moe_w_down': nrm((L, N_EXPERTS, EXPERT_DIM, D), EXPERT_DIM ** -0.5),
        'sh_w_gate': nrm((L, D, SHARED_DIM), D ** -0.5),
        'sh_w_up': nrm((L, D, SHARED_DIM), D ** -0.5),
        'sh_w_down': nrm((L, SHARED_DIM, D), SHARED_DIM ** -0.5),
        'final_g': near_one((D,)),
    }


def reference(x, c, w_ada, b_ada, norm1_g, w_in, rw_mu, rw_w0, rw_w2, rw_a0, rw_a2, rw_g2,
              rw_k_k, rw_k_a, rw_r_k, rw_lnx_g, rw_lnx_b, gm_ln_g, gm_ln_b, gm_w_s, gm_b_s,
              w_br_rwkv, w_br_gmlp, w_out, norm2_g, router_w, router_b, moe_w_gate, moe_w_up,
              moe_w_down, sh_w_gate, sh_w_up, sh_w_down, final_g):
    Bb, Ss, D = x.shape
    for l in range(DEPTH):
        mod = (jax.nn.silu(c) @ w_ada[l] + b_ada[l]).reshape(Bb, 6, D)
        sh1, sc1, gt1 = mod[:, 0, None, :], mod[:, 1, None, :], mod[:, 2, None, :]
        sh2, sc2, gt2 = mod[:, 3, None, :], mod[:, 4, None, :], mod[:, 5, None, :]

        h = rmsnorm(x, norm1_g[l]) * (1.0 + sc1) + sh1
        p = h @ w_in[l]
        p_rw = p[..., :RW_COLS]
        p_gm = p[..., RW_COLS:RW_COLS + 2 * GM_WIDTH]
        p_gt = p[..., RW_COLS + 2 * GM_WIDTH:]
        o_rw = rwkv7_branch(p_rw, rw_mu[l], rw_w0[l], rw_w2[l], rw_a0[l], rw_a2[l], rw_g2[l],
                            rw_k_k[l], rw_k_a[l], rw_r_k[l], rw_lnx_g[l], rw_lnx_b[l]).astype(x.dtype)
        o_gm = gmlp_branch(p_gm, gm_ln_g[l], gm_ln_b[l], gm_w_s[l], gm_b_s[l]).astype(x.dtype)
        gates = jax.nn.sigmoid(p_gt.astype(jnp.float32)).astype(x.dtype)
        merged = gates[..., :D] * (o_rw @ w_br_rwkv[l]) + gates[..., D:] * (o_gm @ w_br_gmlp[l])
        x = x + gt1 * (merged @ w_out[l])

        h2 = rmsnorm(x, norm2_g[l]) * (1.0 + sc2) + sh2
        x = x + gt2 * moe(h2, router_w[l], router_b[l], moe_w_gate[l], moe_w_up[l], moe_w_down[l],
                          sh_w_gate[l], sh_w_up[l], sh_w_down[l])
    return rmsnorm(x, final_g)
```

```python
import functools

import jax
import jax.numpy as jnp
from jax import lax
from jax.experimental import pallas as pl
from jax.experimental.pallas import tpu as pltpu

F32 = jnp.float32
BF16 = jnp.bfloat16
I32 = jnp.int32
U32 = jnp.uint32

D_MODEL = 1024
RW_WIDTH = 512
RW_HEADS = 8
RW_HEAD_DIM = 64
RW_DECAY_LORA = 64
RW_AAA_LORA = 64
RW_GATE_LORA = 128
RW_COLS = 3 * RW_WIDTH + RW_DECAY_LORA + RW_AAA_LORA + RW_GATE_LORA
RW_LNX_EPS = 64e-5
GM_WIDTH = 512
GM_GROUP_DIM = 64
GM_GROUPS = 8
GM_CHUNK = 128
N_EXPERTS = 256
TOP_K = 8
EXPERT_DIM = 256
SHARED_DIM = 256
ROUTED_SCALE = 2.5
NORM_EPS = 1e-6
LN_EPS = 1e-5

LANES = 128
VMEM_LIMIT = 56 * 1024 * 1024

WKV_CHUNK = 64
WKV_QUAD = 4
QW = WKV_QUAD * RW_HEAD_DIM
RW_BLOCK = 256
IN_TM = 512
GM_TM = 512
MG_TM = 256
EXP_BM = 256
DSP_TM = 512
CMB_TM = 128


def _dot(a, b):
    return jnp.dot(a, b, preferred_element_type=F32)


def _dot_nt(a, b):
    return lax.dot_general(a, b, (((1,), (1,)), ((), ())), preferred_element_type=F32)


def _dot_tn(a, b):
    return lax.dot_general(a, b, (((0,), (0,)), ((), ())), preferred_element_type=F32)


def _split2(a):
    hi = a.astype(BF16)
    lo = (a - hi.astype(F32)).astype(BF16)
    return hi, lo


def _split3(a):
    h1 = a.astype(BF16)
    r1 = a - h1.astype(F32)
    h2 = r1.astype(BF16)
    h3 = (r1 - h2.astype(F32)).astype(BF16)
    return h1, h2, h3


def _dot_hp(a, b):
    ah, al = _split2(a)
    bh, bl = _split2(b)
    return _dot(ah, bh) + _dot(al, bh) + _dot(ah, bl)


def _dot_exact_rhs(a, b_bf16):
    h1, h2, h3 = _split3(a)
    return _dot(h1, b_bf16) + _dot(h2, b_bf16) + _dot(h3, b_bf16)


def _sigmoid(x):
    return 1.0 / (1.0 + jnp.exp(-x))


def _rmsnorm_rows(x, g):
    return x * lax.rsqrt(jnp.mean(x * x, axis=-1, keepdims=True) + NORM_EPS) * g


def _params(sem):
    return pltpu.CompilerParams(dimension_semantics=sem, vmem_limit_bytes=VMEM_LIMIT)


def _ada_kernel(c_ref, w_ref, b_ref, o_ref):
    c = c_ref[...]
    s = c * _sigmoid(c)
    o_ref[...] = _dot_hp(s, w_ref[...]) + b_ref[...]


def _ada(c, w, b):
    bsz, d = c.shape
    n = w.shape[1]
    tn = 1024
    return pl.pallas_call(
        _ada_kernel,
        out_shape=jax.ShapeDtypeStruct((bsz, n), F32),
        grid=(n // tn,),
        in_specs=[pl.BlockSpec((bsz, d), lambda j: (0, 0)),
                  pl.BlockSpec((d, tn), lambda j: (0, j)),
                  pl.BlockSpec((1, tn), lambda j: (0, j))],
        out_specs=pl.BlockSpec((bsz, tn), lambda j: (0, j)),
        compiler_params=_params(("arbitrary",)),
    )(c, w, b.reshape(1, n))


def _inproj_kernel(x_ref, mod_ref, g_ref, w_ref, prw_ref, gm_ref, gt_ref):
    mod = mod_ref[0]
    h = _rmsnorm_rows(x_ref[...], g_ref[...]) * (1.0 + mod[1:2]) + mod[0:1]
    hb = h.astype(BF16)
    prw_ref[...] = _dot(hb, w_ref[:, :RW_COLS])
    pgm = _dot(hb, w_ref[:, RW_COLS:RW_COLS + 2 * GM_WIDTH])
    gm_ref[...] = jax.nn.gelu(pgm, approximate=True).astype(BF16)
    pgt = _dot(hb, w_ref[:, RW_COLS + 2 * GM_WIDTH:])
    gt_ref[...] = _sigmoid(pgt).astype(BF16)


def _inproj(x2, mod, g1, w_in_bf, seq):
    n, d = x2.shape
    tm = IN_TM
    per_b = seq // tm
    cols = w_in_bf.shape[1]
    return pl.pallas_call(
        _inproj_kernel,
        out_shape=(jax.ShapeDtypeStruct((n, RW_COLS), F32),
                   jax.ShapeDtypeStruct((n, 2 * GM_WIDTH), BF16),
                   jax.ShapeDtypeStruct((n, 2 * d), BF16)),
        grid=(n // tm,),
        in_specs=[pl.BlockSpec((tm, d), lambda i: (i, 0)),
                  pl.BlockSpec((1, 6, d), lambda i: (i // per_b, 0, 0)),
                  pl.BlockSpec((1, d), lambda i: (0, 0)),
                  pl.BlockSpec((d, cols), lambda i: (0, 0), pipeline_mode=pl.Buffered(1))],
        out_specs=(pl.BlockSpec((tm, RW_COLS), lambda i: (i, 0)),
                   pl.BlockSpec((tm, 2 * GM_WIDTH), lambda i: (i, 0)),
                   pl.BlockSpec((tm, 2 * d), lambda i: (i, 0))),
        compiler_params=_params(("arbitrary",)),
    )(x2, mod, g1.reshape(1, d), w_in_bf)


def _rwkv_kernel(p_ref, mu_ref, w0_ref, w2_ref, a0_ref, a2_ref, g2_ref, kk_ref, ka_ref,
                 rk_ref, lg_ref, lb_ref, e_ref, o_ref, prev_ref, s_ref):
    tb = RW_BLOCK
    t = WKV_CHUNK
    c = RW_WIDTH

    @pl.when(pl.program_id(1) == 0)
    def _():
        prev_ref[...] = jnp.zeros_like(prev_ref)
        s_ref[...] = jnp.zeros_like(s_ref)

    p = p_ref[...]
    row = lax.broadcasted_iota(I32, p.shape, 0)
    prev = jnp.where(row == 0, prev_ref[...], pltpu.roll(p, 1, 0))
    prev_ref[...] = p[tb - 1:tb, :]
    xs = p + (prev - p) * mu_ref[...]
    r = xs[:, 0:c]
    k = xs[:, c:2 * c]
    v = xs[:, 2 * c:3 * c]
    xwa = xs[:, 3 * c:3 * c + LANES]
    xg = xs[:, 3 * c + LANES:3 * c + 2 * LANES]

    wl = w0_ref[...] + _dot_hp(jnp.tanh(xwa), w2_ref[...])
    nz = -wl
    w_log = -(jnp.maximum(nz, 0.0) + jnp.log(1.0 + jnp.exp(-jnp.abs(nz)))) - 0.5
    lw = -jnp.exp(w_log)
    a_sig = _sigmoid(a0_ref[...] + _dot_hp(xwa, a2_ref[...]))
    g = _dot(_sigmoid(xg).astype(BF16), g2_ref[...].astype(BF16))

    e = e_ref[...]

    def headsum(z):
        hi, lo = _split2(z)
        return _dot(hi, e) + _dot(lo, e)

    kk = k * kk_ref[...]
    kk = kk * lax.rsqrt(jnp.maximum(headsum(kk * kk), 1e-24))
    k2 = k * (1.0 + (a_sig - 1.0) * ka_ref[...])
    a_v = -kk
    b_v = kk * a_sig

    ri = lax.broadcasted_iota(I32, (tb, tb), 0)
    ci = lax.broadcasted_iota(I32, (tb, tb), 1)
    tri = jnp.where((ri // t == ci // t) & (ri >= ci), 1.0, 0.0).astype(BF16)
    cum = _dot_exact_rhs_lhs(tri, lw)

    w_abs = jnp.exp(cum)
    at = (a_v * jnp.exp(cum - lw)).astype(BF16)
    rt = (r * w_abs).astype(BF16)
    w_inv = jnp.exp(-cum)
    kh = (k2 * w_inv).astype(BF16)
    bh = (b_v * w_inv).astype(BF16)
    vb = v.astype(BF16)

    lane = lax.broadcasted_iota(I32, (1, QW), 1)
    hmask = [jnp.where(lane // RW_HEAD_DIM == h, 1.0, 0.0).astype(BF16) for h in range(WKV_QUAD)]

    def stack_heads(z):
        return jnp.concatenate([z * m for m in hmask], axis=0)

    n4 = WKV_QUAD * t
    r4 = lax.broadcasted_iota(I32, (n4, n4), 0)
    c4 = lax.broadcasted_iota(I32, (n4, n4), 1)
    same = (r4 // t) == (c4 // t)
    m_strict = same & (r4 > c4)
    m_incl = same & (r4 >= c4)
    eye = jnp.where(r4 == c4, 1.0, 0.0)

    y_rows = []
    for ch in range(tb // t):
        rs = slice(ch * t, (ch + 1) * t)
        cum_end = cum[(ch + 1) * t - 1:(ch + 1) * t, :]
        w_rem = jnp.exp(cum_end - cum[rs, :])
        kw = (k2[rs, :] * w_rem).astype(BF16)
        bw = (b_v[rs, :] * w_rem).astype(BF16)
        w_tot = jnp.exp(cum_end)
        y_quads = []
        for q in range(RW_HEADS // WKV_QUAD):
            cs = slice(q * QW, (q + 1) * QW)
            xa = stack_heads(at[rs, cs])
            xr = stack_heads(rt[rs, cs])
            yk = stack_heads(kh[rs, cs])
            yb = stack_heads(bh[rs, cs])
            vs = stack_heads(vb[rs, cs])
            ykw = stack_heads(kw[:, cs])
            ybw = stack_heads(bw[:, cs])
            a_ak = jnp.where(m_strict, _dot_nt(xa, yk), 0.0).astype(BF16)
            a_ab = jnp.where(m_strict, _dot_nt(xa, yb), 0.0)
            a_rk = jnp.where(m_incl, _dot_nt(xr, yk), 0.0).astype(BF16)
            a_rb = jnp.where(m_incl, _dot_nt(xr, yb), 0.0).astype(BF16)
            pw = a_ab
            tinv = eye + a_ab
            for _ in range(5):
                pw = _dot_hp(pw, pw)
                tinv = tinv + _dot_hp(tinv, pw)
            s_q = s_ref[q]
            s_hi, s_lo = _split2(s_q)
            rhs = _dot_nt(xa, s_hi) + _dot_nt(xa, s_lo) + _dot(a_ak, vs)
            u = _dot_hp(tinv, rhs)
            ub = u.astype(BF16)
            yy = _dot_nt(xr, s_hi) + _dot_nt(xr, s_lo) + _dot(a_rk, vs) + _dot(a_rb, ub)
            y_quads.append(yy[0:t] + yy[t:2 * t] + yy[2 * t:3 * t] + yy[3 * t:4 * t])
            s_ref[q] = s_q * w_tot[:, cs] + _dot_tn(vs, ykw) + _dot_tn(ub, ybw)
        y_rows.append(jnp.concatenate(y_quads, axis=1))
    y = jnp.concatenate(y_rows, axis=0)

    inv_n = 1.0 / RW_HEAD_DIM
    m = headsum(y) * inv_n
    dlt = y - m
    var = headsum(dlt * dlt) * inv_n
    yn = dlt * lax.rsqrt(var + RW_LNX_EPS) * lg_ref[...] + lb_ref[...]
    bonus = headsum(r * k2 * rk_ref[...]) * v
    o_ref[...] = ((yn + bonus) * g).astype(BF16)


def _dot_exact_rhs_lhs(tri_bf16, a):
    h1, h2, h3 = _split3(a)
    return _dot(tri_bf16, h1) + _dot(tri_bf16, h2) + _dot(tri_bf16, h3)


def _rwkv(p_rw, bsz, seq, mu, w0, w2, a0, a2, g2, k_k, k_a, r_k, lnx_g, lnx_b):
    c = RW_WIDTH
    tb = RW_BLOCK
    nblk = seq // tb
    zeros = jnp.zeros((RW_DECAY_LORA, c), F32)
    w2p = jnp.concatenate([w2, zeros], axis=0)
    a2p = jnp.concatenate([zeros, a2], axis=0)
    hid = jnp.arange(c, dtype=I32) // RW_HEAD_DIM
    e = (hid[:, None] == hid[None, :]).astype(BF16)
    row = lambda z: z.reshape(1, -1)
    const = lambda shape: pl.BlockSpec(shape, lambda b, j: (0,) * len(shape))
    return pl.pallas_call(
        _rwkv_kernel,
        out_shape=jax.ShapeDtypeStruct((bsz * seq, c), BF16),
        grid=(bsz, nblk),
        in_specs=[pl.BlockSpec((tb, RW_COLS), lambda b, j: (b * nblk + j, 0)),
                  const((1, RW_COLS)), const((1, c)), const((LANES, c)), const((1, c)),
                  const((LANES, c)), const((RW_GATE_LORA, c)), const((1, c)), const((1, c)),
                  const((1, c)), const((1, c)), const((1, c)), const((c, c))],
        out_specs=pl.BlockSpec((tb, c), lambda b, j: (b * nblk + j, 0)),
        scratch_shapes=[pltpu.VMEM((1, RW_COLS), F32),
                        pltpu.VMEM((RW_HEADS // WKV_QUAD, QW, QW), F32)],
        compiler_params=_params(("arbitrary", "arbitrary")),
    )(p_rw, row(mu), row(w0), w2p, row(a0), a2p, g2, row(k_k), row(k_a), row(r_k),
      row(lnx_g), row(lnx_b), e)


def _gmlp_kernel(z_ref, lg_ref, lb_ref, ws_ref, bst_ref, e_ref, o_ref):
    w = GM_WIDTH
    ch = GM_CHUNK
    z = z_ref[...].astype(F32)
    u = z[:, :w]
    v = z[:, w:]
    e = e_ref[...]
    inv_n = 1.0 / GM_GROUP_DIM

    def groupsum(x):
        hi, lo = _split2(x)
        return _dot(hi, e) + _dot(lo, e)

    m = groupsum(v) * inv_n
    dlt = v - m
    var = groupsum(dlt * dlt) * inv_n
    vn = (dlt * lax.rsqrt(var + LN_EPS) * lg_ref[...] + lb_ref[...]).astype(BF16)

    ri = lax.broadcasted_iota(I32, (ch, ch), 0)
    ci = lax.broadcasted_iota(I32, (ch, ch), 1)
    low = ri >= ci
    lane = lax.broadcasted_iota(I32, (1, LANES), 1)
    m_lo = jnp.where(lane < GM_GROUP_DIM, 1.0, 0.0).astype(BF16)
    m_hi = jnp.where(lane >= GM_GROUP_DIM, 1.0, 0.0).astype(BF16)
    bst = bst_ref[...]
    for pr in range(GM_GROUPS // 2):
        g0, g1 = 2 * pr, 2 * pr + 1
        wcat = jnp.concatenate([jnp.where(low, ws_ref[g0], 0.0), jnp.where(low, ws_ref[g1], 0.0)],
                               axis=1).astype(BF16)
        bias = jnp.where(lane < GM_GROUP_DIM, bst[:, g0:g0 + 1], bst[:, g1:g1 + 1])
        ls = slice(pr * LANES, (pr + 1) * LANES)
        for cc in range(z.shape[0] // ch):
            rs = slice(cc * ch, (cc + 1) * ch)
            vp = vn[rs, ls]
            rhs = jnp.concatenate([vp * m_lo, vp * m_hi], axis=0)
            sv = _dot(wcat, rhs) + bias
            o_ref[rs, ls] = (u[rs, ls] * sv).astype(BF16)


def _gmlp(gm, ln_g, ln_b, w_s, b_s):
    n = gm.shape[0]
    w = GM_WIDTH
    tm = GM_TM
    gid = jnp.arange(w, dtype=I32) // GM_GROUP_DIM
    e = (gid[:, None] == gid[None, :]).astype(BF16)
    return pl.pallas_call(
        _gmlp_kernel,
        out_shape=jax.ShapeDtypeStruct((n, w), BF16),
        grid=(n // tm,),
        in_specs=[pl.BlockSpec((tm, 2 * w), lambda i: (i, 0)),
                  pl.BlockSpec((1, w), lambda i: (0, 0)),
                  pl.BlockSpec((1, w), lambda i: (0, 0)),
                  pl.BlockSpec((GM_GROUPS, GM_CHUNK, GM_CHUNK), lambda i: (0, 0, 0)),
                  pl.BlockSpec((GM_CHUNK, GM_GROUPS), lambda i: (0, 0)),
                  pl.BlockSpec((w, w), lambda i: (0, 0))],
        out_specs=pl.BlockSpec((tm, w), lambda i: (i, 0)),
        compiler_params=_params(("arbitrary",)),
    )(gm, ln_g.reshape(1, w), ln_b.reshape(1, w), w_s, b_s.T, e)


def _merge_kernel(x_ref, orw_ref, ogm_ref, gt_ref, mod_ref, wr_ref, wg_ref, wo_ref, g2_ref,
                  rw_ref, rb_ref, sg_ref, su_ref, sd_ref,
                  base_ref, h2p_ref, idx_ref, pos_ref, wts_ref, cnt_ref):
    d = D_MODEL
    tm = MG_TM
    ne = N_EXPERTS

    @pl.when(pl.program_id(0) == 0)
    def _():
        cnt_ref[...] = jnp.zeros_like(cnt_ref)

    mod = mod_ref[0]
    gt = gt_ref[...]
    br = _dot(orw_ref[...], wr_ref[...])
    bg = _dot(ogm_ref[...], wg_ref[...])
    merged = gt[:, :d].astype(F32) * br + gt[:, d:].astype(F32) * bg
    x1 = x_ref[...] + mod[2:3] * _dot(merged.astype(BF16), wo_ref[...])
    h2 = _rmsnorm_rows(x1, g2_ref[...]) * (1.0 + mod[4:5]) + mod[3:4]

    scores = _sigmoid(_dot_hp(h2, rw_ref[...]))
    cur = scores + rb_ref[...]
    lane_e = lax.broadcasted_iota(I32, (tm, ne), 1).astype(F32)
    ri = lax.broadcasted_iota(I32, (tm, tm), 0)
    ci = lax.broadcasted_iota(I32, (tm, tm), 1)
    tri = jnp.where(ri > ci, 1.0, 0.0).astype(BF16)
    onehots, sels, ixs = [], [], []
    for _ in range(TOP_K):
        mx = jnp.max(cur, axis=-1, keepdims=True)
        ix = jnp.min(jnp.where(cur == mx, lane_e, float(ne)), axis=-1, keepdims=True)
        oh = lane_e == ix
        sels.append(jnp.sum(jnp.where(oh, scores, 0.0), axis=-1, keepdims=True))
        cur = jnp.where(oh, -jnp.inf, cur)
        onehots.append(oh)
        ixs.append(ix)
    chosen = onehots[0]
    for oh in onehots[1:]:
        chosen = chosen | oh
    cmask = jnp.where(chosen, 1.0, 0.0)
    denom = sels[0]
    for s in sels[1:]:
        denom = denom + s
    scale = ROUTED_SCALE / denom
    rank = cnt_ref[0:1, :] + _dot(tri, cmask.astype(BF16))
    cnt_ref[0:1, :] = cnt_ref[0:1, :] + jnp.sum(cmask, axis=0, keepdims=True)
    lane_o = lax.broadcasted_iota(I32, (tm, LANES), 1)
    idx_o = jnp.zeros((tm, LANES), F32)
    pos_o = jnp.zeros((tm, LANES), F32)
    wts_o = jnp.zeros((tm, LANES), F32)
    for kslot in range(TOP_K):
        pos_k = jnp.sum(jnp.where(onehots[kslot], rank, 0.0), axis=-1, keepdims=True)
        here = lane_o == kslot
        idx_o = jnp.where(here, ixs[kslot], idx_o)
        pos_o = jnp.where(here, pos_k, pos_o)
        wts_o = jnp.where(here, sels[kslot] * scale, wts_o)
    idx_ref[...] = idx_o.astype(I32)
    pos_ref[...] = pos_o.astype(I32)
    wts_ref[...] = wts_o

    hb = h2.astype(BF16)
    sgate = _dot(hb, sg_ref[...])
    act = (sgate * _sigmoid(sgate) * _dot(hb, su_ref[...])).astype(BF16)
    base_ref[...] = x1 + mod[5:6] * _dot(act, sd_ref[...])

    lo = lax.bitcast_convert_type(hb[:, :d // 2].astype(F32), U32) >> 16
    hi = lax.bitcast_convert_type(hb[:, d // 2:].astype(F32), U32) & jnp.uint32(0xFFFF0000)
    h2p_ref[...] = lo | hi


def _merge(x2, o_rw, o_gm, gt, mod, seq, wr, wg, wo, g2n, router_w, router_b, sg, su, sd):
    n, d = x2.shape
    tm = MG_TM
    per_b = seq // tm
    ne = N_EXPERTS
    tile = lambda w: pl.BlockSpec((tm, w), lambda i: (i, 0))
    const2 = lambda a, b: pl.BlockSpec((a, b), lambda i: (0, 0))
    return pl.pallas_call(
        _merge_kernel,
        out_shape=(jax.ShapeDtypeStruct((n, d), F32),
                   jax.ShapeDtypeStruct((n, d // 2), U32),
                   jax.ShapeDtypeStruct((n, LANES), I32),
                   jax.ShapeDtypeStruct((n, LANES), I32),
                   jax.ShapeDtypeStruct((n, LANES), F32),
                   jax.ShapeDtypeStruct((8, ne), F32)),
        grid=(n // tm,),
        in_specs=[tile(d), tile(RW_WIDTH), tile(GM_WIDTH), tile(2 * d),
                  pl.BlockSpec((1, 6, d), lambda i: (i // per_b, 0, 0)),
                  const2(RW_WIDTH, d), const2(GM_WIDTH, d), const2(d, d), const2(1, d),
                  const2(d, ne), const2(1, ne),
                  const2(d, SHARED_DIM), const2(d, SHARED_DIM), const2(SHARED_DIM, d)],
        out_specs=(tile(d), tile(d // 2), tile(LANES), tile(LANES), tile(LANES),
                   pl.BlockSpec((8, ne), lambda i: (0, 0))),
        compiler_params=_params(("arbitrary",)),
    )(x2, o_rw, o_gm, gt, mod, wr, wg, wo, g2n.reshape(1, d), router_w, router_b.reshape(1, ne),
      sg, su, sd)


def _dispatch_kernel(dest_ref, h_ref, xs_ref, sem):
    tm = DSP_TM

    def row_copy(tok, d):
        return pltpu.make_async_copy(h_ref.at[pl.ds(tok, 1)], xs_ref.at[pl.ds(d, 1)], sem)

    def issue(tok, carry):
        for kslot in range(TOP_K):
            row_copy(tok, dest_ref[tok * TOP_K + kslot]).start()
        return carry

    lax.fori_loop(0, tm, issue, 0)

    def drain(tok, carry):
        for kslot in range(TOP_K):
            row_copy(tok, dest_ref[tok * TOP_K + kslot]).wait()
        return carry

    lax.fori_loop(0, tm, drain, 0)


def _dispatch(dest_flat, h2p, p_rows):
    n, w = h2p.shape
    tm = DSP_TM
    return pl.pallas_call(
        _dispatch_kernel,
        out_shape=jax.ShapeDtypeStruct((p_rows, w), U32),
        grid=(n // tm,),
        in_specs=[pl.BlockSpec((tm * TOP_K,), lambda i: (i,), memory_space=pltpu.SMEM),
                  pl.BlockSpec((tm, w), lambda i: (i, 0))],
        out_specs=pl.BlockSpec(memory_space=pl.ANY),
        scratch_shapes=[pltpu.SemaphoreType.DMA(())],
        compiler_params=_params(("arbitrary",)),
    )(dest_flat, h2p)


def _experts_kernel(be_ref, nb_ref, xs_ref, wg_ref, wu_ref, wd_ref, o_ref, wgb, wub, wdb):
    i = pl.program_id(0)

    @pl.when(i < nb_ref[0])
    def _():
        prev_e = be_ref[jnp.maximum(i - 1, 0)]

        @pl.when((i == 0) | (be_ref[i] != prev_e))
        def _():
            wgb[...] = wg_ref[0].astype(BF16)
            wub[...] = wu_ref[0].astype(BF16)
            wdb[...] = wd_ref[0].astype(BF16)

        w = xs_ref[...]
        xa = lax.bitcast_convert_type(w << 16, F32).astype(BF16)
        xb = lax.bitcast_convert_type(w & jnp.uint32(0xFFFF0000), F32).astype(BF16)
        half = D_MODEL // 2
        gate = _dot(xa, wgb[:half, :]) + _dot(xb, wgb[half:, :])
        up = _dot(xa, wub[:half, :]) + _dot(xb, wub[half:, :])
        act = (gate * _sigmoid(gate) * up).astype(BF16)
        o_ref[...] = _dot(act, wdb[...])


def _experts(block_e, nblk_used, xs, w_gate, w_up, w_down):
    p_rows, w = xs.shape
    bm = EXP_BM
    d = D_MODEL
    nblk = p_rows // bm
    last = lambda i, be, nb: jnp.minimum(i, nb[0] - 1)
    grid_spec = pltpu.PrefetchScalarGridSpec(
        num_scalar_prefetch=2,
        grid=(nblk,),
        in_specs=[pl.BlockSpec((bm, w), lambda i, be, nb: (last(i, be, nb), 0)),
                  pl.BlockSpec((1, d, EXPERT_DIM), lambda i, be, nb: (be[last(i, be, nb)], 0, 0)),
                  pl.BlockSpec((1, d, EXPERT_DIM), lambda i, be, nb: (be[last(i, be, nb)], 0, 0)),
                  pl.BlockSpec((1, EXPERT_DIM, d), lambda i, be, nb: (be[last(i, be, nb)], 0, 0))],
        out_specs=pl.BlockSpec((bm, d), lambda i, be, nb: (last(i, be, nb), 0)),
        scratch_shapes=[pltpu.VMEM((d, EXPERT_DIM), BF16),
                        pltpu.VMEM((d, EXPERT_DIM), BF16),
                        pltpu.VMEM((EXPERT_DIM, d), BF16)],
    )
    return pl.pallas_call(
        _experts_kernel,
        out_shape=jax.ShapeDtypeStruct((p_rows, d), F32),
        grid_spec=grid_spec,
        compiler_params=_params(("arbitrary",)),
    )(block_e, nblk_used, xs, w_gate, w_up, w_down)


def _combine_kernel(dest_ref, wts_ref, base_ref, mod_ref, fg_ref, ob_ref, o_ref, buf, sem):
    tm = CMB_TM

    def row_copy(tok, kslot, d):
        return pltpu.make_async_copy(ob_ref.at[pl.ds(d, 1)], buf.at[kslot, pl.ds(tok, 1)], sem)

    def issue(tok, carry):
        for kslot in range(TOP_K):
            row_copy(tok, kslot, dest_ref[tok * TOP_K + kslot]).start()
        return carry

    lax.fori_loop(0, tm, issue, 0)

    def drain(tok, carry):
        for kslot in range(TOP_K):
            row_copy(tok, kslot, dest_ref[tok * TOP_K + kslot]).wait()
        return carry

    lax.fori_loop(0, tm, drain, 0)

    wts = wts_ref[...]
    acc = wts[:, 0:1] * buf[0]
    for kslot in range(1, TOP_K):
        acc = acc + wts[:, kslot:kslot + 1] * buf[kslot]
    x2 = base_ref[...] + mod_ref[0][5:6] * acc
    o_ref[...] = _rmsnorm_rows(x2, fg_ref[...])


def _combine(dest_flat, wts, base, mod, seq, final_g, ob):
    n, d = base.shape
    tm = CMB_TM
    per_b = seq // tm
    return pl.pallas_call(
        _combine_kernel,
        out_shape=jax.ShapeDtypeStruct((n, d), F32),
        grid=(n // tm,),
        in_specs=[pl.BlockSpec((tm * TOP_K,), lambda i: (i,), memory_space=pltpu.SMEM),
                  pl.BlockSpec((tm, LANES), lambda i: (i, 0)),
                  pl.BlockSpec((tm, d), lambda i: (i, 0)),
                  pl.BlockSpec((1, 6, d), lambda i: (i // per_b, 0, 0)),
                  pl.BlockSpec((1, d), lambda i: (0, 0)),
                  pl.BlockSpec(memory_space=pl.ANY)],
        out_specs=pl.BlockSpec((tm, d), lambda i: (i, 0)),
        scratch_shapes=[pltpu.VMEM((TOP_K, tm, d), F32), pltpu.SemaphoreType.DMA(())],
        compiler_params=_params(("arbitrary",)),
    )(dest_flat, wts, base, mod, final_g.reshape(1, d), ob)


def _layer(x2, bsz, seq, c, w_ada, b_ada, norm1_g, w_in, rw_mu, rw_w0, rw_w2, rw_a0, rw_a2, rw_g2,
           rw_k_k, rw_k_a, rw_r_k, rw_lnx_g, rw_lnx_b, gm_ln_g, gm_ln_b, gm_w_s, gm_b_s,
           w_br_rwkv, w_br_gmlp, w_out, norm2_g, router_w, router_b, moe_w_gate, moe_w_up,
           moe_w_down, sh_w_gate, sh_w_up, sh_w_down, final_g):
    n, d = x2.shape
    mod = _ada(c, w_ada, b_ada).reshape(bsz, 6, d)
    p_rw, gm, gt = _inproj(x2, mod, norm1_g, w_in.astype(BF16), seq)
    o_rw = _rwkv(p_rw, bsz, seq, rw_mu, rw_w0, rw_w2, rw_a0, rw_a2, rw_g2, rw_k_k, rw_k_a,
                 rw_r_k, rw_lnx_g, rw_lnx_b)
    o_gm = _gmlp(gm, gm_ln_g, gm_ln_b, gm_w_s, gm_b_s)
    base, h2p, idx, pos, wts, cnt = _merge(
        x2, o_rw, o_gm, gt, mod, seq, w_br_rwkv.astype(BF16), w_br_gmlp.astype(BF16),
        w_out.astype(BF16), norm2_g, router_w, router_b, sh_w_gate.astype(BF16),
        sh_w_up.astype(BF16), sh_w_down.astype(BF16))

    bm = EXP_BM
    counts = cnt[0].astype(I32)
    padded = (counts + bm - 1) // bm * bm
    pends = jnp.cumsum(padded)
    pstarts = pends - padded
    p_rows = n * TOP_K + N_EXPERTS * bm
    nblk = p_rows // bm
    dest = (pstarts[idx[:, :TOP_K]] + pos[:, :TOP_K]).reshape(-1)
    block_e = jnp.minimum(
        jnp.searchsorted(pends, jnp.arange(nblk, dtype=I32) * bm, side='right'),
        N_EXPERTS - 1).astype(I32)
    nblk_used = (pends[-1:] // bm).astype(I32)

    xs = _dispatch(dest, h2p, p_rows)
    ob = _experts(block_e, nblk_used, xs, moe_w_gate, moe_w_up, moe_w_down)
    return _combine(dest, wts, base, mod, seq, final_g, ob)


def kernel(x, c, w_ada, b_ada, norm1_g, w_in, rw_mu, rw_w0, rw_w2, rw_a0, rw_a2, rw_g2, rw_k_k, rw_k_a, rw_r_k, rw_lnx_g, rw_lnx_b, gm_ln_g, gm_ln_b, gm_w_s, gm_b_s, w_br_rwkv, w_br_gmlp, w_out, norm2_g, router_w, router_b, moe_w_gate, moe_w_up, moe_w_down, sh_w_gate, sh_w_up, sh_w_down, final_g):
    bsz, seq, d = x.shape
    assert d == D_MODEL and w_ada.shape[0] == 1, "single-layer block of width D_MODEL"
    assert seq % IN_TM == 0 and seq % RW_BLOCK == 0 and seq % GM_TM == 0 and seq % MG_TM == 0
    out = _layer(x.reshape(bsz * seq, d), bsz, seq, c, w_ada[0], b_ada[0], norm1_g[0], w_in[0],
                 rw_mu[0], rw_w0[0], rw_w2[0], rw_a0[0], rw_a2[0], rw_g2[0], rw_k_k[0], rw_k_a[0],
                 rw_r_k[0].reshape(-1), rw_lnx_g[0], rw_lnx_b[0], gm_ln_g[0].reshape(-1),
                 gm_ln_b[0].reshape(-1), gm_w_s[0], gm_b_s[0], w_br_rwkv[0], w_br_gmlp[0],
                 w_out[0], norm2_g[0], router_w[0], router_b[0], moe_w_gate[0], moe_w_up[0],
                 moe_w_down[0], sh_w_gate[0], sh_w_up[0], sh_w_down[0], final_g)
    return out.reshape(bsz, seq, d)
```

```python
import functools

import jax
import jax.numpy as jnp
from jax import lax
from jax.experimental import pallas as pl
from jax.experimental.pallas import tpu as pltpu

F32 = jnp.float32
BF16 = jnp.bfloat16
I32 = jnp.int32

D_MODEL = 1024
RW_WIDTH = 512
RW_HEADS = 8
RW_HEAD_DIM = 64
RW_DECAY_LORA = 64
RW_AAA_LORA = 64
RW_GATE_LORA = 128
RW_COLS = 3 * RW_WIDTH + RW_DECAY_LORA + RW_AAA_LORA + RW_GATE_LORA
RW_LNX_EPS = 64e-5
GM_WIDTH = 512
GM_GROUP_DIM = 64
GM_GROUPS = 8
GM_CHUNK = 128
N_EXPERTS = 256
TOP_K = 8
EXPERT_DIM = 256
SHARED_DIM = 256
ROUTED_SCALE = 2.5
NORM_EPS = 1e-6
LN_EPS = 1e-5

LANES = 128
VMEM_LIMIT = 56 * 1024 * 1024

WKV_CHUNK = 64
WKV_QUAD = 4
QW = WKV_QUAD * RW_HEAD_DIM
RW_BLOCK = 256
IN_TM = 512
GM_TM = 512
MG_TM = 256
EXP_BM = 256
DSP_TM = 1024
CMB_TM = 128


def _dot(a, b):
    return jnp.dot(a, b, preferred_element_type=F32)


def _dot_nt(a, b):
    return lax.dot_general(a, b, (((1,), (1,)), ((), ())), preferred_element_type=F32)


def _dot_tn(a, b):
    return lax.dot_general(a, b, (((0,), (0,)), ((), ())), preferred_element_type=F32)


def _split2(a):
    hi = a.astype(BF16)
    lo = (a - hi.astype(F32)).astype(BF16)
    return hi, lo


def _split3(a):
    h1 = a.astype(BF16)
    r1 = a - h1.astype(F32)
    h2 = r1.astype(BF16)
    h3 = (r1 - h2.astype(F32)).astype(BF16)
    return h1, h2, h3


def _dot_hp(a, b):
    ah, al = _split2(a)
    bh, bl = _split2(b)
    return _dot(ah, bh) + _dot(al, bh) + _dot(ah, bl)


def _dot_exact_rhs(a, b_bf16):
    h1, h2, h3 = _split3(a)
    return _dot(h1, b_bf16) + _dot(h2, b_bf16) + _dot(h3, b_bf16)


def _sigmoid(x):
    return 1.0 / (1.0 + jnp.exp(-x))


def _rmsnorm_rows(x, g):
    return x * lax.rsqrt(jnp.mean(x * x, axis=-1, keepdims=True) + NORM_EPS) * g


def _params(sem):
    return pltpu.CompilerParams(dimension_semantics=sem, vmem_limit_bytes=VMEM_LIMIT)


def _ada_kernel(c_ref, w_ref, b_ref, o_ref):
    c = c_ref[...]
    s = c * _sigmoid(c)
    o_ref[...] = _dot_hp(s, w_ref[...]) + b_ref[...]


def _ada(c, w, b):
    bsz, d = c.shape
    n = w.shape[1]
    tn = 1024
    return pl.pallas_call(
        _ada_kernel,
        out_shape=jax.ShapeDtypeStruct((bsz, n), F32),
        grid=(n // tn,),
        in_specs=[pl.BlockSpec((bsz, d), lambda j: (0, 0)),
                  pl.BlockSpec((d, tn), lambda j: (0, j)),
                  pl.BlockSpec((1, tn), lambda j: (0, j))],
        out_specs=pl.BlockSpec((bsz, tn), lambda j: (0, j)),
        compiler_params=_params(("arbitrary",)),
    )(c, w, b.reshape(1, n))


def _inproj_kernel(x_ref, mod_ref, g_ref, w_ref, prw_ref, gm_ref, gt_ref):
    mod = mod_ref[0]
    h = _rmsnorm_rows(x_ref[...], g_ref[...]) * (1.0 + mod[1:2]) + mod[0:1]
    hb = h.astype(BF16)
    prw_ref[...] = _dot(hb, w_ref[:, :RW_COLS])
    pgm = _dot(hb, w_ref[:, RW_COLS:RW_COLS + 2 * GM_WIDTH])
    gm_ref[...] = jax.nn.gelu(pgm, approximate=True).astype(BF16)
    pgt = _dot(hb, w_ref[:, RW_COLS + 2 * GM_WIDTH:])
    gt_ref[...] = _sigmoid(pgt).astype(BF16)


def _inproj(x2, mod, g1, w_in_bf, seq):
    n, d = x2.shape
    tm = IN_TM
    per_b = seq // tm
    cols = w_in_bf.shape[1]
    return pl.pallas_call(
        _inproj_kernel,
        out_shape=(jax.ShapeDtypeStruct((n, RW_COLS), F32),
                   jax.ShapeDtypeStruct((n, 2 * GM_WIDTH), BF16),
                   jax.ShapeDtypeStruct((n, 2 * d), BF16)),
        grid=(n // tm,),
        in_specs=[pl.BlockSpec((tm, d), lambda i: (i, 0)),
                  pl.BlockSpec((1, 6, d), lambda i: (i // per_b, 0, 0)),
                  pl.BlockSpec((1, d), lambda i: (0, 0)),
                  pl.BlockSpec((d, cols), lambda i: (0, 0), pipeline_mode=pl.Buffered(1))],
        out_specs=(pl.BlockSpec((tm, RW_COLS), lambda i: (i, 0)),
                   pl.BlockSpec((tm, 2 * GM_WIDTH), lambda i: (i, 0)),
                   pl.BlockSpec((tm, 2 * d), lambda i: (i, 0))),
        compiler_params=_params(("arbitrary",)),
    )(x2, mod, g1.reshape(1, d), w_in_bf)


def _rwkv_kernel(p_ref, mu_ref, w0_ref, w2_ref, a0_ref, a2_ref, g2_ref, kk_ref, ka_ref,
                 rk_ref, lg_ref, lb_ref, e_ref, o_ref, prev_ref, s_ref):
    tb = RW_BLOCK
    t = WKV_CHUNK
    c = RW_WIDTH

    @pl.when(pl.program_id(1) == 0)
    def _():
        prev_ref[...] = jnp.zeros_like(prev_ref)
        s_ref[...] = jnp.zeros_like(s_ref)

    p = p_ref[...]
    row = lax.broadcasted_iota(I32, p.shape, 0)
    prev = jnp.where(row == 0, prev_ref[...], pltpu.roll(p, 1, 0))
    prev_ref[...] = p[tb - 1:tb, :]
    xs = p + (prev - p) * mu_ref[...]
    r = xs[:, 0:c]
    k = xs[:, c:2 * c]
    v = xs[:, 2 * c:3 * c]
    xwa = xs[:, 3 * c:3 * c + LANES]
    xg = xs[:, 3 * c + LANES:3 * c + 2 * LANES]

    wl = w0_ref[...] + _dot_hp(jnp.tanh(xwa), w2_ref[...])
    nz = -wl
    w_log = -(jnp.maximum(nz, 0.0) + jnp.log(1.0 + jnp.exp(-jnp.abs(nz)))) - 0.5
    lw = -jnp.exp(w_log)
    a_sig = _sigmoid(a0_ref[...] + _dot_hp(xwa, a2_ref[...]))
    g = _dot(_sigmoid(xg).astype(BF16), g2_ref[...].astype(BF16))

    e = e_ref[...]

    def headsum(z):
        hi, lo = _split2(z)
        return _dot(hi, e) + _dot(lo, e)

    kk = k * kk_ref[...]
    kk = kk * lax.rsqrt(jnp.maximum(headsum(kk * kk), 1e-24))
    k2 = k * (1.0 + (a_sig - 1.0) * ka_ref[...])
    a_v = -kk
    b_v = kk * a_sig

    ri = lax.broadcasted_iota(I32, (tb, tb), 0)
    ci = lax.broadcasted_iota(I32, (tb, tb), 1)
    tri = jnp.where((ri // t == ci // t) & (ri >= ci), 1.0, 0.0).astype(BF16)
    cum = _dot_exact_rhs_lhs(tri, lw)

    w_abs = jnp.exp(cum)
    at = (a_v * jnp.exp(cum - lw)).astype(BF16)
    rt = (r * w_abs).astype(BF16)
    w_inv = jnp.exp(-cum)
    kh = (k2 * w_inv).astype(BF16)
    bh = (b_v * w_inv).astype(BF16)
    vb = v.astype(BF16)

    lane = lax.broadcasted_iota(I32, (1, QW), 1)
    hmask = [jnp.where(lane // RW_HEAD_DIM == h, 1.0, 0.0).astype(BF16) for h in range(WKV_QUAD)]

    def stack_heads(z):
        return jnp.concatenate([z * m for m in hmask], axis=0)

    n4 = WKV_QUAD * t
    r4 = lax.broadcasted_iota(I32, (n4, n4), 0)
    c4 = lax.broadcasted_iota(I32, (n4, n4), 1)
    same = (r4 // t) == (c4 // t)
    m_strict = same & (r4 > c4)
    m_incl = same & (r4 >= c4)
    eye = jnp.where(r4 == c4, 1.0, 0.0)

    y_rows = []
    for ch in range(tb // t):
        rs = slice(ch * t, (ch + 1) * t)
        cum_end = cum[(ch + 1) * t - 1:(ch + 1) * t, :]
        w_rem = jnp.exp(cum_end - cum[rs, :])
        kw = (k2[rs, :] * w_rem).astype(BF16)
        bw = (b_v[rs, :] * w_rem).astype(BF16)
        w_tot = jnp.exp(cum_end)
        y_quads = []
        for q in range(RW_HEADS // WKV_QUAD):
            cs = slice(q * QW, (q + 1) * QW)
            xa = stack_heads(at[rs, cs])
            xr = stack_heads(rt[rs, cs])
            yk = stack_heads(kh[rs, cs])
            yb = stack_heads(bh[rs, cs])
            vs = stack_heads(vb[rs, cs])
            ykw = stack_heads(kw[:, cs])
            ybw = stack_heads(bw[:, cs])
            a_ak = jnp.where(m_strict, _dot_nt(xa, yk), 0.0).astype(BF16)
            a_ab = jnp.where(m_strict, _dot_nt(xa, yb), 0.0)
            a_rk = jnp.where(m_incl, _dot_nt(xr, yk), 0.0).astype(BF16)
            a_rb = jnp.where(m_incl, _dot_nt(xr, yb), 0.0).astype(BF16)
            pw = a_ab.astype(BF16)
            tinv = eye + a_ab
            for _ in range(5):
                pw = _dot(pw, pw).astype(BF16)
                tinv = tinv + _dot(tinv.astype(BF16), pw)
            s_q = s_ref[q]
            sb = s_q.astype(BF16)
            rhs = _dot_nt(xa, sb) + _dot(a_ak, vs)
            ub = _dot(tinv.astype(BF16), rhs.astype(BF16)).astype(BF16)
            yy = _dot_nt(xr, sb) + _dot(a_rk, vs) + _dot(a_rb, ub)
            y_quads.append(yy[0:t] + yy[t:2 * t] + yy[2 * t:3 * t] + yy[3 * t:4 * t])
            s_ref[q] = s_q * w_tot[:, cs] + _dot_tn(vs, ykw) + _dot_tn(ub, ybw)
        y_rows.append(jnp.concatenate(y_quads, axis=1))
    y = jnp.concatenate(y_rows, axis=0)

    inv_n = 1.0 / RW_HEAD_DIM
    m = headsum(y) * inv_n
    dlt = y - m
    var = headsum(dlt * dlt) * inv_n
    yn = dlt * lax.rsqrt(var + RW_LNX_EPS) * lg_ref[...] + lb_ref[...]
    bonus = headsum(r * k2 * rk_ref[...]) * v
    o_ref[...] = ((yn + bonus) * g).astype(BF16)


def _dot_exact_rhs_lhs(tri_bf16, a):
    h1, h2, h3 = _split3(a)
    return _dot(tri_bf16, h1) + _dot(tri_bf16, h2) + _dot(tri_bf16, h3)


def _rwkv(p_rw, bsz, seq, mu, w0, w2, a0, a2, g2, k_k, k_a, r_k, lnx_g, lnx_b):
    c = RW_WIDTH
    tb = RW_BLOCK
    nblk = seq // tb
    zeros = jnp.zeros((RW_DECAY_LORA, c), F32)
    w2p = jnp.concatenate([w2, zeros], axis=0)
    a2p = jnp.concatenate([zeros, a2], axis=0)
    hid = jnp.arange(c, dtype=I32) // RW_HEAD_DIM
    e = (hid[:, None] == hid[None, :]).astype(BF16)
    row = lambda z: z.reshape(1, -1)
    const = lambda shape: pl.BlockSpec(shape, lambda b, j: (0,) * len(shape))
    return pl.pallas_call(
        _rwkv_kernel,
        out_shape=jax.ShapeDtypeStruct((bsz * seq, c), BF16),
        grid=(bsz, nblk),
        in_specs=[pl.BlockSpec((tb, RW_COLS), lambda b, j: (b * nblk + j, 0)),
                  const((1, RW_COLS)), const((1, c)), const((LANES, c)), const((1, c)),
                  const((LANES, c)), const((RW_GATE_LORA, c)), const((1, c)), const((1, c)),
                  const((1, c)), const((1, c)), const((1, c)), const((c, c))],
        out_specs=pl.BlockSpec((tb, c), lambda b, j: (b * nblk + j, 0)),
        scratch_shapes=[pltpu.VMEM((1, RW_COLS), F32),
                        pltpu.VMEM((RW_HEADS // WKV_QUAD, QW, QW), F32)],
        compiler_params=_params(("arbitrary", "arbitrary")),
    )(p_rw, row(mu), row(w0), w2p, row(a0), a2p, g2, row(k_k), row(k_a), row(r_k),
      row(lnx_g), row(lnx_b), e)


def _gmlp_kernel(z_ref, lg_ref, lb_ref, ws_ref, bst_ref, e_ref, o_ref):
    w = GM_WIDTH
    ch = GM_CHUNK
    z = z_ref[...].astype(F32)
    u = z[:, :w]
    v = z[:, w:]
    e = e_ref[...]
    inv_n = 1.0 / GM_GROUP_DIM

    def groupsum(x):
        hi, lo = _split2(x)
        return _dot(hi, e) + _dot(lo, e)

    m = groupsum(v) * inv_n
    dlt = v - m
    var = groupsum(dlt * dlt) * inv_n
    vn = (dlt * lax.rsqrt(var + LN_EPS) * lg_ref[...] + lb_ref[...]).astype(BF16)

    ri = lax.broadcasted_iota(I32, (ch, ch), 0)
    ci = lax.broadcasted_iota(I32, (ch, ch), 1)
    low = ri >= ci
    lane = lax.broadcasted_iota(I32, (1, LANES), 1)
    m_lo = jnp.where(lane < GM_GROUP_DIM, 1.0, 0.0).astype(BF16)
    m_hi = jnp.where(lane >= GM_GROUP_DIM, 1.0, 0.0).astype(BF16)
    bst = bst_ref[...]
    for pr in range(GM_GROUPS // 2):
        g0, g1 = 2 * pr, 2 * pr + 1
        wcat = jnp.concatenate([jnp.where(low, ws_ref[g0], 0.0), jnp.where(low, ws_ref[g1], 0.0)],
                               axis=1).astype(BF16)
        bias = jnp.where(lane < GM_GROUP_DIM, bst[:, g0:g0 + 1], bst[:, g1:g1 + 1])
        ls = slice(pr * LANES, (pr + 1) * LANES)
        for cc in range(z.shape[0] // ch):
            rs = slice(cc * ch, (cc + 1) * ch)
            vp = vn[rs, ls]
            rhs = jnp.concatenate([vp * m_lo, vp * m_hi], axis=0)
            sv = _dot(wcat, rhs) + bias
            o_ref[rs, ls] = (u[rs, ls] * sv).astype(BF16)


def _gmlp(gm, ln_g, ln_b, w_s, b_s):
    n = gm.shape[0]
    w = GM_WIDTH
    tm = GM_TM
    gid = jnp.arange(w, dtype=I32) // GM_GROUP_DIM
    e = (gid[:, None] == gid[None, :]).astype(BF16)
    return pl.pallas_call(
        _gmlp_kernel,
        out_shape=jax.ShapeDtypeStruct((n, w), BF16),
        grid=(n // tm,),
        in_specs=[pl.BlockSpec((tm, 2 * w), lambda i: (i, 0)),
                  pl.BlockSpec((1, w), lambda i: (0, 0)),
                  pl.BlockSpec((1, w), lambda i: (0, 0)),
                  pl.BlockSpec((GM_GROUPS, GM_CHUNK, GM_CHUNK), lambda i: (0, 0, 0)),
                  pl.BlockSpec((GM_CHUNK, GM_GROUPS), lambda i: (0, 0)),
                  pl.BlockSpec((w, w), lambda i: (0, 0))],
        out_specs=pl.BlockSpec((tm, w), lambda i: (i, 0)),
        compiler_params=_params(("arbitrary",)),
    )(gm, ln_g.reshape(1, w), ln_b.reshape(1, w), w_s, b_s.T, e)


def _merge_kernel(x_ref, orw_ref, ogm_ref, gt_ref, mod_ref, wr_ref, wg_ref, wo_ref, g2_ref,
                  rw_ref, rb_ref, sg_ref, su_ref, sd_ref,
                  base_ref, h2_ref, idx_ref, pos_ref, wts_ref, cnt_ref):
    d = D_MODEL
    tm = MG_TM
    ne = N_EXPERTS

    @pl.when(pl.program_id(0) == 0)
    def _():
        cnt_ref[...] = jnp.zeros_like(cnt_ref)

    mod = mod_ref[0]
    gt = gt_ref[...]
    br = _dot(orw_ref[...], wr_ref[...])
    bg = _dot(ogm_ref[...], wg_ref[...])
    merged = gt[:, :d].astype(F32) * br + gt[:, d:].astype(F32) * bg
    x1 = x_ref[...] + mod[2:3] * _dot(merged.astype(BF16), wo_ref[...])
    h2 = _rmsnorm_rows(x1, g2_ref[...]) * (1.0 + mod[4:5]) + mod[3:4]

    scores = _sigmoid(_dot_hp(h2, rw_ref[...]))
    cur = scores + rb_ref[...]
    lane_e = lax.broadcasted_iota(I32, (tm, ne), 1).astype(F32)
    ri = lax.broadcasted_iota(I32, (tm, tm), 0)
    ci = lax.broadcasted_iota(I32, (tm, tm), 1)
    tri = jnp.where(ri > ci, 1.0, 0.0).astype(BF16)
    onehots, sels, ixs = [], [], []
    for _ in range(TOP_K):
        mx = jnp.max(cur, axis=-1, keepdims=True)
        ix = jnp.min(jnp.where(cur == mx, lane_e, float(ne)), axis=-1, keepdims=True)
        oh = lane_e == ix
        sels.append(jnp.sum(jnp.where(oh, scores, 0.0), axis=-1, keepdims=True))
        cur = jnp.where(oh, -jnp.inf, cur)
        onehots.append(oh)
        ixs.append(ix)
    chosen = onehots[0]
    for oh in onehots[1:]:
        chosen = chosen | oh
    cmask = jnp.where(chosen, 1.0, 0.0)
    denom = sels[0]
    for s in sels[1:]:
        denom = denom + s
    scale = ROUTED_SCALE / denom
    rank = cnt_ref[0:1, :] + _dot(tri, cmask.astype(BF16))
    cnt_ref[0:1, :] = cnt_ref[0:1, :] + jnp.sum(cmask, axis=0, keepdims=True)
    lane_o = lax.broadcasted_iota(I32, (tm, LANES), 1)
    idx_o = jnp.zeros((tm, LANES), F32)
    pos_o = jnp.zeros((tm, LANES), F32)
    wts_o = jnp.zeros((tm, LANES), F32)
    for kslot in range(TOP_K):
        pos_k = jnp.sum(jnp.where(onehots[kslot], rank, 0.0), axis=-1, keepdims=True)
        here = lane_o == kslot
        idx_o = jnp.where(here, ixs[kslot], idx_o)
        pos_o = jnp.where(here, pos_k, pos_o)
        wts_o = jnp.where(here, sels[kslot] * scale, wts_o)
    idx_ref[...] = idx_o.astype(I32)
    pos_ref[...] = pos_o.astype(I32)
    wts_ref[...] = wts_o

    hb = h2.astype(BF16)
    sgate = _dot(hb, sg_ref[...])
    act = (sgate * _sigmoid(sgate) * _dot(hb, su_ref[...])).astype(BF16)
    base_ref[...] = x1 + mod[5:6] * _dot(act, sd_ref[...])

    h2_ref[...] = h2


def _merge(x2, o_rw, o_gm, gt, mod, seq, wr, wg, wo, g2n, router_w, router_b, sg, su, sd):
    n, d = x2.shape
    tm = MG_TM
    per_b = seq // tm
    ne = N_EXPERTS
    tile = lambda w: pl.BlockSpec((tm, w), lambda i: (i, 0))
    const2 = lambda a, b: pl.BlockSpec((a, b), lambda i: (0, 0))
    return pl.pallas_call(
        _merge_kernel,
        out_shape=(jax.ShapeDtypeStruct((n, d), F32),
                   jax.ShapeDtypeStruct((n, d), F32),
                   jax.ShapeDtypeStruct((n, LANES), I32),
                   jax.ShapeDtypeStruct((n, LANES), I32),
                   jax.ShapeDtypeStruct((n, LANES), F32),
                   jax.ShapeDtypeStruct((8, ne), F32)),
        grid=(n // tm,),
        in_specs=[tile(d), tile(RW_WIDTH), tile(GM_WIDTH), tile(2 * d),
                  pl.BlockSpec((1, 6, d), lambda i: (i // per_b, 0, 0)),
                  const2(RW_WIDTH, d), const2(GM_WIDTH, d), const2(d, d), const2(1, d),
                  const2(d, ne), const2(1, ne),
                  const2(d, SHARED_DIM), const2(d, SHARED_DIM), const2(SHARED_DIM, d)],
        out_specs=(tile(d), tile(d), tile(LANES), tile(LANES), tile(LANES),
                   pl.BlockSpec((8, ne), lambda i: (0, 0))),
        compiler_params=_params(("arbitrary",)),
    )(x2, o_rw, o_gm, gt, mod, wr, wg, wo, g2n.reshape(1, d), router_w, router_b.reshape(1, ne),
      sg, su, sd)


def _destmap_kernel(idx_ref, pos_ref, ps_ref, o_ref):
    tm = idx_ref.shape[0]
    idx = idx_ref[...]
    ps = ps_ref[...]
    lane_e = lax.broadcasted_iota(I32, (tm, N_EXPERTS), 1)
    lane_o = lax.broadcasted_iota(I32, (tm, LANES), 1)
    start = jnp.zeros((tm, LANES), F32)
    for kslot in range(TOP_K):
        hit = lane_e == idx[:, kslot:kslot + 1]
        s_k = jnp.sum(jnp.where(hit, ps, 0.0), axis=-1, keepdims=True)
        start = jnp.where(lane_o == kslot, s_k, start)
    o_ref[...] = pos_ref[...] + start.astype(I32)


def _destmap(idx, pos, pstarts_f32):
    n = idx.shape[0]
    tm = 1024
    return pl.pallas_call(
        _destmap_kernel,
        out_shape=jax.ShapeDtypeStruct((n, LANES), I32),
        grid=(n // tm,),
        in_specs=[pl.BlockSpec((tm, LANES), lambda i: (i, 0)),
                  pl.BlockSpec((tm, LANES), lambda i: (i, 0)),
                  pl.BlockSpec((1, N_EXPERTS), lambda i: (0, 0))],
        out_specs=pl.BlockSpec((tm, LANES), lambda i: (i, 0)),
        compiler_params=_params(("arbitrary",)),
    )(idx, pos, pstarts_f32.reshape(1, N_EXPERTS))


def _dispatch_kernel(zf_ref, dest_ref, h_ref, xs_ref, zbuf, sem):
    tm = DSP_TM
    bm = EXP_BM

    @pl.when(pl.program_id(0) == 0)
    def _():
        zbuf[...] = jnp.zeros_like(zbuf)

        def zero_copy(e):
            return pltpu.make_async_copy(zbuf, xs_ref.at[pl.ds(pl.multiple_of(zf_ref[e], bm), bm)], sem)

        def zstart(e, carry):
            @pl.when(zf_ref[e] >= 0)
            def _():
                zero_copy(e).start()
            return carry

        def zwait(e, carry):
            @pl.when(zf_ref[e] >= 0)
            def _():
                zero_copy(e).wait()
            return carry

        lax.fori_loop(0, 2 * N_EXPERTS, zstart, 0)
        lax.fori_loop(0, 2 * N_EXPERTS, zwait, 0)

    def issue(tok, carry):
        for kslot in range(TOP_K):
            pltpu.make_async_copy(h_ref.at[pl.ds(tok, 1)],
                                  xs_ref.at[pl.ds(dest_ref[tok * TOP_K + kslot], 1)],
                                  sem).start(priority=kslot % 2)
        return carry

    lax.fori_loop(0, tm, issue, 0, unroll=4)
    for kslot in range(TOP_K):
        pltpu.make_async_copy(h_ref, xs_ref.at[pl.ds(0, tm)], sem).wait()


def _dispatch(zfill, dest_flat, h2f, p_rows):
    n, w = h2f.shape
    tm = DSP_TM
    grid_spec = pltpu.PrefetchScalarGridSpec(
        num_scalar_prefetch=1,
        grid=(n // tm,),
        in_specs=[pl.BlockSpec((tm * TOP_K,), lambda i, zf: (i,), memory_space=pltpu.SMEM),
                  pl.BlockSpec((tm, w), lambda i, zf: (i, 0))],
        out_specs=pl.BlockSpec(memory_space=pl.ANY),
        scratch_shapes=[pltpu.VMEM((EXP_BM, w), F32), pltpu.SemaphoreType.DMA(())],
    )
    return pl.pallas_call(
        _dispatch_kernel,
        out_shape=jax.ShapeDtypeStruct((p_rows, w), F32),
        grid_spec=grid_spec,
        compiler_params=_params(("arbitrary",)),
    )(zfill, dest_flat, h2f)


def _experts_kernel(be_ref, nb_ref, xs_ref, wg_ref, wu_ref, wd_ref, o_ref, wgb, wub, wdb):
    i = pl.program_id(0)

    @pl.when(i >= nb_ref[0])
    def _():
        o_ref[...] = jnp.zeros_like(o_ref)

    @pl.when(i < nb_ref[0])
    def _():
        prev_e = be_ref[jnp.maximum(i - 1, 0)]

        @pl.when((i == 0) | (be_ref[i] != prev_e))
        def _():
            wgb[...] = wg_ref[0].astype(BF16)
            wub[...] = wu_ref[0].astype(BF16)
            wdb[...] = wd_ref[0].astype(BF16)

        xb = xs_ref[...].astype(BF16)
        gate = _dot(xb, wgb[...])
        up = _dot(xb, wub[...])
        act = (gate * _sigmoid(gate) * up).astype(BF16)
        o_ref[...] = _dot(act, wdb[...])


def _experts(block_e, nblk_used, xs, w_gate, w_up, w_down):
    p_rows, w = xs.shape
    bm = EXP_BM
    d = D_MODEL
    nblk = p_rows // bm
    last = lambda i, be, nb: jnp.minimum(i, nb[0] - 1)
    grid_spec = pltpu.PrefetchScalarGridSpec(
        num_scalar_prefetch=2,
        grid=(nblk,),
        in_specs=[pl.BlockSpec((bm, w), lambda i, be, nb: (last(i, be, nb), 0)),
                  pl.BlockSpec((1, d, EXPERT_DIM), lambda i, be, nb: (be[last(i, be, nb)], 0, 0)),
                  pl.BlockSpec((1, d, EXPERT_DIM), lambda i, be, nb: (be[last(i, be, nb)], 0, 0)),
                  pl.BlockSpec((1, EXPERT_DIM, d), lambda i, be, nb: (be[last(i, be, nb)], 0, 0))],
        out_specs=pl.BlockSpec((bm, d), lambda i, be, nb: (i, 0)),
        scratch_shapes=[pltpu.VMEM((d, EXPERT_DIM), BF16),
                        pltpu.VMEM((d, EXPERT_DIM), BF16),
                        pltpu.VMEM((EXPERT_DIM, d), BF16)],
    )
    return pl.pallas_call(
        _experts_kernel,
        out_shape=jax.ShapeDtypeStruct((p_rows, d), F32),
        grid_spec=grid_spec,
        compiler_params=_params(("arbitrary",)),
    )(block_e, nblk_used, xs, w_gate, w_up, w_down)


def _combine_kernel(dcur_ref, dnxt_ref, wts_ref, base_ref, mod_ref, fg_ref, ob_ref, o_ref, buf, sems):
    tm = CMB_TM
    i = pl.program_id(0)
    slot = lax.rem(i, 2)

    def issue(d_ref, s):
        def body(tok, carry):
            for kslot in range(TOP_K):
                pltpu.make_async_copy(ob_ref.at[pl.ds(d_ref[tok * TOP_K + kslot], 1)],
                                      buf.at[s, kslot, pl.ds(tok, 1)],
                                      sems.at[s]).start(priority=kslot % 2)
            return carry
        lax.fori_loop(0, tm, body, 0, unroll=4)

    @pl.when(i == 0)
    def _():
        issue(dcur_ref, 0)

    @pl.when(i + 1 < pl.num_programs(0))
    def _():
        issue(dnxt_ref, 1 - slot)

    for kslot in range(TOP_K):
        pltpu.make_async_copy(ob_ref.at[pl.ds(0, tm)], buf.at[slot, kslot], sems.at[slot]).wait()

    wts = wts_ref[...]
    acc = wts[:, 0:1] * buf[slot, 0]
    for kslot in range(1, TOP_K):
        acc = acc + wts[:, kslot:kslot + 1] * buf[slot, kslot]
    x2 = base_ref[...] + mod_ref[0][5:6] * acc
    o_ref[...] = _rmsnorm_rows(x2, fg_ref[...])


def _combine(dest_flat, wts, base, mod, seq, final_g, ob):
    n, d = base.shape
    tm = CMB_TM
    per_b = seq // tm
    nt = n // tm
    return pl.pallas_call(
        _combine_kernel,
        out_shape=jax.ShapeDtypeStruct((n, d), F32),
        grid=(nt,),
        in_specs=[pl.BlockSpec((tm * TOP_K,), lambda i: (i,), memory_space=pltpu.SMEM),
                  pl.BlockSpec((tm * TOP_K,), lambda i: (jnp.minimum(i + 1, nt - 1),),
                               memory_space=pltpu.SMEM),
                  pl.BlockSpec((tm, LANES), lambda i: (i, 0)),
                  pl.BlockSpec((tm, d), lambda i: (i, 0)),
                  pl.BlockSpec((1, 6, d), lambda i: (i // per_b, 0, 0)),
                  pl.BlockSpec((1, d), lambda i: (0, 0)),
                  pl.BlockSpec(memory_space=pl.ANY)],
        out_specs=pl.BlockSpec((tm, d), lambda i: (i, 0)),
        scratch_shapes=[pltpu.VMEM((2, TOP_K, tm, d), F32), pltpu.SemaphoreType.DMA((2,))],
        compiler_params=_params(("arbitrary",)),
    )(dest_flat, dest_flat, wts, base, mod, final_g.reshape(1, d), ob)


def _layer(x2, bsz, seq, c, w_ada, b_ada, norm1_g, w_in, rw_mu, rw_w0, rw_w2, rw_a0, rw_a2, rw_g2,
           rw_k_k, rw_k_a, rw_r_k, rw_lnx_g, rw_lnx_b, gm_ln_g, gm_ln_b, gm_w_s, gm_b_s,
           w_br_rwkv, w_br_gmlp, w_out, norm2_g, router_w, router_b, moe_w_gate, moe_w_up,
           moe_w_down, sh_w_gate, sh_w_up, sh_w_down, final_g):
    n, d = x2.shape
    mod = _ada(c, w_ada, b_ada).reshape(bsz, 6, d)
    p_rw, gm, gt = _inproj(x2, mod, norm1_g, w_in.astype(BF16), seq)
    o_rw = _rwkv(p_rw, bsz, seq, rw_mu, rw_w0, rw_w2, rw_a0, rw_a2, rw_g2, rw_k_k, rw_k_a,
                 rw_r_k, rw_lnx_g, rw_lnx_b)
    o_gm = _gmlp(gm, gm_ln_g, gm_ln_b, gm_w_s, gm_b_s)
    base, h2f, idx, pos, wts, cnt = _merge(
        x2, o_rw, o_gm, gt, mod, seq, w_br_rwkv.astype(BF16), w_br_gmlp.astype(BF16),
        w_out.astype(BF16), norm2_g, router_w, router_b, sh_w_gate.astype(BF16),
        sh_w_up.astype(BF16), sh_w_down.astype(BF16))

    bm = EXP_BM
    counts = cnt[0].astype(I32)
    padded = (counts + bm - 1) // bm * bm
    pends = jnp.cumsum(padded)
    pstarts = pends - padded
    p_rows = n * TOP_K + N_EXPERTS * bm
    nblk = p_rows // bm
    dest = _destmap(idx, pos, pstarts.astype(F32))[:, :TOP_K].reshape(-1)
    row0 = jnp.arange(nblk, dtype=I32) * bm
    block_e = jnp.minimum(jnp.sum((pends[None, :] <= row0[:, None]).astype(I32), axis=1),
                          N_EXPERTS - 1)
    nblk_used = (pends[-1:] // bm).astype(I32)
    tail = nblk_used[0] + jnp.arange(N_EXPERTS, dtype=I32)
    zfill = jnp.concatenate([jnp.where(padded > 0, pends - bm, -1),
                             jnp.where(tail < nblk, tail * bm, -1)]).astype(I32)

    xs = _dispatch(zfill, dest, h2f, p_rows)
    ob = _experts(block_e, nblk_used, xs, moe_w_gate, moe_w_up, moe_w_down)
    return _combine(dest, wts, base, mod, seq, final_g, ob)


def kernel(x, c, w_ada, b_ada, norm1_g, w_in, rw_mu, rw_w0, rw_w2, rw_a0, rw_a2, rw_g2, rw_k_k, rw_k_a, rw_r_k, rw_lnx_g, rw_lnx_b, gm_ln_g, gm_ln_b, gm_w_s, gm_b_s, w_br_rwkv, w_br_gmlp, w_out, norm2_g, router_w, router_b, moe_w_gate, moe_w_up, moe_w_down, sh_w_gate, sh_w_up, sh_w_down, final_g):
    bsz, seq, d = x.shape
    assert d == D_MODEL and w_ada.shape[0] == 1, "single-layer block of width D_MODEL"
    assert seq % IN_TM == 0 and seq % RW_BLOCK == 0 and seq % GM_TM == 0 and seq % MG_TM == 0
    out = _layer(x.reshape(bsz * seq, d), bsz, seq, c, w_ada[0], b_ada[0], norm1_g[0], w_in[0],
                 rw_mu[0], rw_w0[0], rw_w2[0], rw_a0[0], rw_a2[0], rw_g2[0], rw_k_k[0], rw_k_a[0],
                 rw_r_k[0].reshape(-1), rw_lnx_g[0], rw_lnx_b[0], gm_ln_g[0].reshape(-1),
                 gm_ln_b[0].reshape(-1), gm_w_s[0], gm_b_s[0], w_br_rwkv[0], w_br_gmlp[0],
                 w_out[0], norm2_g[0], router_w[0], router_b[0], moe_w_gate[0], moe_w_up[0],
                 moe_w_down[0], sh_w_gate[0], sh_w_up[0], sh_w_down[0], final_g)
    return out.reshape(bsz, seq, d)
```

```python
import functools

import jax
import jax.numpy as jnp
from jax import lax
from jax.experimental import pallas as pl
from jax.experimental.pallas import tpu as pltpu

F32 = jnp.float32
BF16 = jnp.bfloat16
I32 = jnp.int32

D_MODEL = 1024
RW_WIDTH = 512
RW_HEADS = 8
RW_HEAD_DIM = 64
RW_DECAY_LORA = 64
RW_AAA_LORA = 64
RW_GATE_LORA = 128
RW_COLS = 3 * RW_WIDTH + RW_DECAY_LORA + RW_AAA_LORA + RW_GATE_LORA
RW_LNX_EPS = 64e-5
GM_WIDTH = 512
GM_GROUP_DIM = 64
GM_GROUPS = 8
GM_CHUNK = 128
N_EXPERTS = 256
TOP_K = 8
EXPERT_DIM = 256
SHARED_DIM = 256
ROUTED_SCALE = 2.5
NORM_EPS = 1e-6
LN_EPS = 1e-5

LANES = 128
VMEM_LIMIT = 56 * 1024 * 1024

WKV_CHUNK = 64
WKV_QUAD = 4
QW = WKV_QUAD * RW_HEAD_DIM
RW_BLOCK = 256
RW_NB = 2
IN_TM = 512
GM_TM = 512
MG_TM = 256
EXP_BM = 256
DSP_TM = 1024
CMB_TM = 128


def _dot(a, b):
    return jnp.dot(a, b, preferred_element_type=F32)


def _dot_nt(a, b):
    return lax.dot_general(a, b, (((1,), (1,)), ((), ())), preferred_element_type=F32)


def _dot_tn(a, b):
    return lax.dot_general(a, b, (((0,), (0,)), ((), ())), preferred_element_type=F32)


def _split2(a):
    hi = a.astype(BF16)
    lo = (a - hi.astype(F32)).astype(BF16)
    return hi, lo


def _split3(a):
    h1 = a.astype(BF16)
    r1 = a - h1.astype(F32)
    h2 = r1.astype(BF16)
    h3 = (r1 - h2.astype(F32)).astype(BF16)
    return h1, h2, h3


def _dot_hp(a, b):
    ah, al = _split2(a)
    bh, bl = _split2(b)
    return _dot(ah, bh) + _dot(al, bh) + _dot(ah, bl)


def _dot_exact_rhs(a, b_bf16):
    h1, h2, h3 = _split3(a)
    return _dot(h1, b_bf16) + _dot(h2, b_bf16) + _dot(h3, b_bf16)


def _sigmoid(x):
    return 1.0 / (1.0 + jnp.exp(-x))


def _rmsnorm_rows(x, g):
    return x * lax.rsqrt(jnp.mean(x * x, axis=-1, keepdims=True) + NORM_EPS) * g


def _params(sem):
    return pltpu.CompilerParams(dimension_semantics=sem, vmem_limit_bytes=VMEM_LIMIT)


def _ada_kernel(c_ref, w_ref, b_ref, o_ref):
    c = c_ref[...]
    s = c * _sigmoid(c)
    o_ref[...] = _dot_hp(s, w_ref[...]) + b_ref[...]


def _ada(c, w, b):
    bsz, d = c.shape
    n = w.shape[1]
    tn = 1024
    return pl.pallas_call(
        _ada_kernel,
        out_shape=jax.ShapeDtypeStruct((bsz, n), F32),
        grid=(n // tn,),
        in_specs=[pl.BlockSpec((bsz, d), lambda j: (0, 0)),
                  pl.BlockSpec((d, tn), lambda j: (0, j)),
                  pl.BlockSpec((1, tn), lambda j: (0, j))],
        out_specs=pl.BlockSpec((bsz, tn), lambda j: (0, j)),
        compiler_params=_params(("arbitrary",)),
    )(c, w, b.reshape(1, n))


def _inproj_kernel(x_ref, mod_ref, g_ref, w_ref, prw_ref, gm_ref, gt_ref):
    mod = mod_ref[0]
    h = _rmsnorm_rows(x_ref[...], g_ref[...]) * (1.0 + mod[1:2]) + mod[0:1]
    hb = h.astype(BF16)
    prw_ref[...] = _dot(hb, w_ref[:, :RW_COLS])
    pgm = _dot(hb, w_ref[:, RW_COLS:RW_COLS + 2 * GM_WIDTH])
    gm_ref[...] = jax.nn.gelu(pgm, approximate=True).astype(BF16)
    pgt = _dot(hb, w_ref[:, RW_COLS + 2 * GM_WIDTH:])
    gt_ref[...] = _sigmoid(pgt).astype(BF16)


def _inproj(x2, mod, g1, w_in_bf, seq):
    n, d = x2.shape
    tm = IN_TM
    per_b = seq // tm
    cols = w_in_bf.shape[1]
    return pl.pallas_call(
        _inproj_kernel,
        out_shape=(jax.ShapeDtypeStruct((n, RW_COLS), F32),
                   jax.ShapeDtypeStruct((n, 2 * GM_WIDTH), BF16),
                   jax.ShapeDtypeStruct((n, 2 * d), BF16)),
        grid=(n // tm,),
        in_specs=[pl.BlockSpec((tm, d), lambda i: (i, 0)),
                  pl.BlockSpec((1, 6, d), lambda i: (i // per_b, 0, 0)),
                  pl.BlockSpec((1, d), lambda i: (0, 0)),
                  pl.BlockSpec((d, cols), lambda i: (0, 0), pipeline_mode=pl.Buffered(1))],
        out_specs=(pl.BlockSpec((tm, RW_COLS), lambda i: (i, 0)),
                   pl.BlockSpec((tm, 2 * GM_WIDTH), lambda i: (i, 0)),
                   pl.BlockSpec((tm, 2 * d), lambda i: (i, 0))),
        compiler_params=_params(("arbitrary",)),
    )(x2, mod, g1.reshape(1, d), w_in_bf)


def _rwkv_kernel(p_ref, mu_ref, w0_ref, w2_ref, a0_ref, a2_ref, g2_ref, kk_ref, ka_ref,
                 rk_ref, lg_ref, lb_ref, e_ref, o_ref, prev_ref, s_ref):
    nb = RW_NB
    tb = RW_BLOCK
    t = WKV_CHUNK
    c = RW_WIDTH
    nq = RW_HEADS // WKV_QUAD
    nch = tb // t

    @pl.when(pl.program_id(1) == 0)
    def _():
        prev_ref[...] = jnp.zeros_like(prev_ref)
        s_ref[...] = jnp.zeros_like(s_ref)

    p = p_ref[...].reshape(nb * tb, RW_COLS)
    row = lax.broadcasted_iota(I32, p.shape, 0)
    prev = pltpu.roll(p, 1, 0)
    for bi in range(nb):
        prev = jnp.where(row == bi * tb, prev_ref[bi:bi + 1, :], prev)
        prev_ref[bi:bi + 1, :] = p[(bi + 1) * tb - 1:(bi + 1) * tb, :]
    xs = p + (prev - p) * mu_ref[...]
    r = xs[:, 0:c]
    k = xs[:, c:2 * c]
    v = xs[:, 2 * c:3 * c]
    xwa = xs[:, 3 * c:3 * c + LANES]
    xg = xs[:, 3 * c + LANES:3 * c + 2 * LANES]

    wl = w0_ref[...] + _dot_hp(jnp.tanh(xwa), w2_ref[...])
    nz = -wl
    w_log = -(jnp.maximum(nz, 0.0) + jnp.log(1.0 + jnp.exp(-jnp.abs(nz)))) - 0.5
    lw = -jnp.exp(w_log)
    a_sig = _sigmoid(a0_ref[...] + _dot_hp(xwa, a2_ref[...]))
    g = _dot(_sigmoid(xg).astype(BF16), g2_ref[...].astype(BF16))

    e = e_ref[...]

    def headsum(z):
        hi, lo = _split2(z)
        return _dot(hi, e) + _dot(lo, e)

    kk = k * kk_ref[...]
    kk = kk * lax.rsqrt(jnp.maximum(headsum(kk * kk), 1e-24))
    k2 = k * (1.0 + (a_sig - 1.0) * ka_ref[...])
    a_v = -kk
    b_v = kk * a_sig

    ri = lax.broadcasted_iota(I32, (tb, tb), 0)
    ci = lax.broadcasted_iota(I32, (tb, tb), 1)
    tri = jnp.where((ri // t == ci // t) & (ri >= ci), 1.0, 0.0).astype(BF16)
    cum = jnp.concatenate([_dot_exact_lhs(tri, lw[bi * tb:(bi + 1) * tb, :]) for bi in range(nb)], axis=0)
    ends = [cum[(j + 1) * t - 1:(j + 1) * t, :] for j in range(nb * nch)]
    cum_end = jnp.concatenate([jnp.broadcast_to(z, (t, c)) for z in ends], axis=0)

    w_abs = jnp.exp(cum)
    at = (a_v * jnp.exp(cum - lw)).astype(BF16)
    rt = (r * w_abs).astype(BF16)
    w_inv = jnp.exp(-cum)
    kh = (k2 * w_inv).astype(BF16)
    bh = (b_v * w_inv).astype(BF16)
    vb = v.astype(BF16)
    w_rem = jnp.exp(cum_end - cum)
    kw = (k2 * w_rem).astype(BF16)
    bw = (b_v * w_rem).astype(BF16)

    lane = lax.broadcasted_iota(I32, (1, QW), 1)
    hmask = [jnp.where(lane // RW_HEAD_DIM == h, 1.0, 0.0).astype(BF16) for h in range(WKV_QUAD)]

    def stack_heads(z):
        return jnp.concatenate([z * m for m in hmask], axis=0)

    tt = lax.broadcasted_iota(I32, (t, QW), 0)
    ss = lax.broadcasted_iota(I32, (t, QW), 1) % t
    m_strict = tt > ss
    m_incl = tt >= ss
    eye = jnp.where(tt == ss, 1.0, 0.0)
    r4 = lax.broadcasted_iota(I32, (QW, QW), 0)
    c4 = lax.broadcasted_iota(I32, (QW, QW), 1)
    bdm = jnp.where((r4 // t) == (c4 // t), 1.0, 0.0)

    keys = [(bi, ch, q) for bi in range(nb) for ch in range(nch) for q in range(nq)]
    sl = {(bi, ch, q): (slice(bi * tb + ch * t, bi * tb + (ch + 1) * t), slice(q * QW, (q + 1) * QW))
          for bi, ch, q in keys}
    ar = {key: jnp.concatenate([at[sl[key]], rt[sl[key]]], axis=0) for key in keys}
    aa = {key: _dot_nt(ar[key], jnp.concatenate([stack_heads(kh[sl[key]]), stack_heads(bh[sl[key]])],
                                                 axis=0))
          for key in keys}
    a_ak = {key: jnp.where(m_strict, aa[key][0:t, 0:QW], 0.0).astype(BF16) for key in keys}
    a_ab = {key: jnp.where(m_strict, aa[key][0:t, QW:2 * QW], 0.0) for key in keys}
    a_rk = {key: jnp.where(m_incl, aa[key][t:2 * t, 0:QW], 0.0).astype(BF16) for key in keys}
    a_rb = {key: jnp.where(m_incl, aa[key][t:2 * t, QW:2 * QW], 0.0).astype(BF16) for key in keys}
    tinv = {key: eye + a_ab[key] for key in keys}
    pw = {}
    for key in keys:
        lb = a_ab[key].astype(BF16)
        pw[key] = _dot(lb, stack_heads(lb)).astype(BF16)
    for _ in range(4):
        both = {key: _dot(jnp.concatenate([pw[key], tinv[key].astype(BF16)], axis=0), stack_heads(pw[key]))
                for key in keys}
        tinv = {key: tinv[key] + both[key][t:2 * t] for key in keys}
        pw = {key: both[key][0:t].astype(BF16) for key in keys}
    tinv = {key: (tinv[key] + _dot(tinv[key].astype(BF16), stack_heads(pw[key]))).astype(BF16)
            for key in keys}
    avv = {key: _dot(jnp.concatenate([a_ak[key], a_rk[key]], axis=0), stack_heads(vb[sl[key]]))
           for key in keys}

    y_blk = {}
    for ch in range(nch):
        ks = [(bi, ch, q) for bi in range(nb) for q in range(nq)]
        s_q = {key: s_ref[key[0] * nq + key[2]] for key in ks}
        sst = {key: _dot_nt(ar[key], s_q[key].astype(BF16)) for key in ks}
        ub = {key: _dot(tinv[key], stack_heads((sst[key][0:t] + avv[key][0:t]).astype(BF16))).astype(BF16)
              for key in ks}
        upd = {key: _dot_tn(jnp.concatenate([vb[sl[key]], ub[key]], axis=0),
                            jnp.concatenate([kw[sl[key]], bw[sl[key]]], axis=0)) for key in ks}
        for key in ks:
            rs, cs = sl[key]
            w_tot = jnp.exp(ends[key[0] * nch + ch][:, cs])
            s_ref[key[0] * nq + key[2]] = s_q[key] * w_tot + upd[key] * bdm
        for key in ks:
            y_blk[key] = sst[key][t:2 * t] + avv[key][t:2 * t] + _dot(a_rb[key], stack_heads(ub[key]))
    y = jnp.concatenate([jnp.concatenate([y_blk[bi, ch, q] for q in range(nq)], axis=1)
                         for bi in range(nb) for ch in range(nch)], axis=0)

    inv_n = 1.0 / RW_HEAD_DIM
    m = headsum(y) * inv_n
    dlt = y - m
    var = headsum(dlt * dlt) * inv_n
    yn = dlt * lax.rsqrt(var + RW_LNX_EPS) * lg_ref[...] + lb_ref[...]
    bonus = headsum(r * k2 * rk_ref[...]) * v
    o_ref[...] = ((yn + bonus) * g).astype(BF16).reshape(nb, tb, c)


def _dot_exact_lhs(tri_bf16, a):
    h1, h2, h3 = _split3(a)
    return _dot(tri_bf16, h1) + _dot(tri_bf16, h2) + _dot(tri_bf16, h3)


def _rwkv(p_rw, bsz, seq, mu, w0, w2, a0, a2, g2, k_k, k_a, r_k, lnx_g, lnx_b):
    c = RW_WIDTH
    tb = RW_BLOCK
    nb = RW_NB
    zeros = jnp.zeros((RW_DECAY_LORA, c), F32)
    w2p = jnp.concatenate([w2, zeros], axis=0)
    a2p = jnp.concatenate([zeros, a2], axis=0)
    hid = jnp.arange(c, dtype=I32) // RW_HEAD_DIM
    e = (hid[:, None] == hid[None, :]).astype(BF16)
    row = lambda z: z.reshape(1, -1)
    const = lambda shape: pl.BlockSpec(shape, lambda b, j: (0,) * len(shape))
    out = pl.pallas_call(
        _rwkv_kernel,
        out_shape=jax.ShapeDtypeStruct((bsz, seq, c), BF16),
        grid=(bsz // nb, seq // tb),
        in_specs=[pl.BlockSpec((nb, tb, RW_COLS), lambda b, j: (b, j, 0)),
                  const((1, RW_COLS)), const((1, c)), const((LANES, c)), const((1, c)),
                  const((LANES, c)), const((RW_GATE_LORA, c)), const((1, c)), const((1, c)),
                  const((1, c)), const((1, c)), const((1, c)), const((c, c))],
        out_specs=pl.BlockSpec((nb, tb, c), lambda b, j: (b, j, 0)),
        scratch_shapes=[pltpu.VMEM((nb, RW_COLS), F32),
                        pltpu.VMEM((nb * (RW_HEADS // WKV_QUAD), QW, QW), F32)],
        compiler_params=_params(("arbitrary", "arbitrary")),
    )(p_rw.reshape(bsz, seq, RW_COLS), row(mu), row(w0), w2p, row(a0), a2p, g2, row(k_k), row(k_a),
      row(r_k), row(lnx_g), row(lnx_b), e)
    return out.reshape(bsz * seq, c)


def _gmlp_kernel(z_ref, lg_ref, lb_ref, ws_ref, bst_ref, e_ref, o_ref):
    w = GM_WIDTH
    ch = GM_CHUNK
    z = z_ref[...].astype(F32)
    u = z[:, :w]
    v = z[:, w:]
    e = e_ref[...]
    inv_n = 1.0 / GM_GROUP_DIM

    def groupsum(x):
        hi, lo = _split2(x)
        return _dot(hi, e) + _dot(lo, e)

    m = groupsum(v) * inv_n
    dlt = v - m
    var = groupsum(dlt * dlt) * inv_n
    vn = (dlt * lax.rsqrt(var + LN_EPS) * lg_ref[...] + lb_ref[...]).astype(BF16)

    ri = lax.broadcasted_iota(I32, (ch, ch), 0)
    ci = lax.broadcasted_iota(I32, (ch, ch), 1)
    low = ri >= ci
    lane = lax.broadcasted_iota(I32, (1, LANES), 1)
    m_lo = jnp.where(lane < GM_GROUP_DIM, 1.0, 0.0).astype(BF16)
    m_hi = jnp.where(lane >= GM_GROUP_DIM, 1.0, 0.0).astype(BF16)
    bst = bst_ref[...]
    for pr in range(GM_GROUPS // 2):
        g0, g1 = 2 * pr, 2 * pr + 1
        wcat = jnp.concatenate([jnp.where(low, ws_ref[g0], 0.0), jnp.where(low, ws_ref[g1], 0.0)],
                               axis=1).astype(BF16)
        bias = jnp.where(lane < GM_GROUP_DIM, bst[:, g0:g0 + 1], bst[:, g1:g1 + 1])
        ls = slice(pr * LANES, (pr + 1) * LANES)
        for cc in range(z.shape[0] // ch):
            rs = slice(cc * ch, (cc + 1) * ch)
            vp = vn[rs, ls]
            rhs = jnp.concatenate([vp * m_lo, vp * m_hi], axis=0)
            sv = _dot(wcat, rhs) + bias
            o_ref[rs, ls] = (u[rs, ls] * sv).astype(BF16)


def _gmlp(gm, ln_g, ln_b, w_s, b_s):
    n = gm.shape[0]
    w = GM_WIDTH
    tm = GM_TM
    gid = jnp.arange(w, dtype=I32) // GM_GROUP_DIM
    e = (gid[:, None] == gid[None, :]).astype(BF16)
    return pl.pallas_call(
        _gmlp_kernel,
        out_shape=jax.ShapeDtypeStruct((n, w), BF16),
        grid=(n // tm,),
        in_specs=[pl.BlockSpec((tm, 2 * w), lambda i: (i, 0)),
                  pl.BlockSpec((1, w), lambda i: (0, 0)),
                  pl.BlockSpec((1, w), lambda i: (0, 0)),
                  pl.BlockSpec((GM_GROUPS, GM_CHUNK, GM_CHUNK), lambda i: (0, 0, 0)),
                  pl.BlockSpec((GM_CHUNK, GM_GROUPS), lambda i: (0, 0)),
                  pl.BlockSpec((w, w), lambda i: (0, 0))],
        out_specs=pl.BlockSpec((tm, w), lambda i: (i, 0)),
        compiler_params=_params(("arbitrary",)),
    )(gm, ln_g.reshape(1, w), ln_b.reshape(1, w), w_s, b_s.T, e)


def _merge_kernel(x_ref, orw_ref, ogm_ref, gt_ref, mod_ref, wr_ref, wg_ref, wo_ref, g2_ref,
                  rw_ref, rb_ref, sg_ref, su_ref, sd_ref,
                  base_ref, h2_ref, idx_ref, pos_ref, wts_ref, cnt_ref):
    d = D_MODEL
    tm = MG_TM
    ne = N_EXPERTS

    @pl.when(pl.program_id(0) == 0)
    def _():
        cnt_ref[...] = jnp.zeros_like(cnt_ref)

    mod = mod_ref[0]
    gt = gt_ref[...]
    br = _dot(orw_ref[...], wr_ref[...])
    bg = _dot(ogm_ref[...], wg_ref[...])
    merged = gt[:, :d].astype(F32) * br + gt[:, d:].astype(F32) * bg
    x1 = x_ref[...] + mod[2:3] * _dot(merged.astype(BF16), wo_ref[...])
    h2 = _rmsnorm_rows(x1, g2_ref[...]) * (1.0 + mod[4:5]) + mod[3:4]

    scores = _sigmoid(_dot_hp(h2, rw_ref[...]))
    cur = scores + rb_ref[...]
    lane_e = lax.broadcasted_iota(I32, (tm, ne), 1).astype(F32)
    ri = lax.broadcasted_iota(I32, (tm, tm), 0)
    ci = lax.broadcasted_iota(I32, (tm, tm), 1)
    tri = jnp.where(ri > ci, 1.0, 0.0).astype(BF16)
    onehots, sels, ixs = [], [], []
    for _ in range(TOP_K):
        mx = jnp.max(cur, axis=-1, keepdims=True)
        ix = jnp.min(jnp.where(cur == mx, lane_e, float(ne)), axis=-1, keepdims=True)
        oh = lane_e == ix
        sels.append(jnp.sum(jnp.where(oh, scores, 0.0), axis=-1, keepdims=True))
        cur = jnp.where(oh, -jnp.inf, cur)
        onehots.append(oh)
        ixs.append(ix)
    chosen = onehots[0]
    for oh in onehots[1:]:
        chosen = chosen | oh
    cmask = jnp.where(chosen, 1.0, 0.0)
    denom = sels[0]
    for s in sels[1:]:
        denom = denom + s
    scale = ROUTED_SCALE / denom
    rank = cnt_ref[0:1, :] + _dot(tri, cmask.astype(BF16))
    cnt_ref[0:1, :] = cnt_ref[0:1, :] + jnp.sum(cmask, axis=0, keepdims=True)
    lane_o = lax.broadcasted_iota(I32, (tm, LANES), 1)
    idx_o = jnp.zeros((tm, LANES), F32)
    pos_o = jnp.zeros((tm, LANES), F32)
    wts_o = jnp.zeros((tm, LANES), F32)
    for kslot in range(TOP_K):
        pos_k = jnp.sum(jnp.where(onehots[kslot], rank, 0.0), axis=-1, keepdims=True)
        here = lane_o == kslot
        idx_o = jnp.where(here, ixs[kslot], idx_o)
        pos_o = jnp.where(here, pos_k, pos_o)
        wts_o = jnp.where(here, sels[kslot] * scale, wts_o)
    idx_ref[...] = idx_o.astype(I32)
    pos_ref[...] = pos_o.astype(I32)
    wts_ref[...] = wts_o

    hb = h2.astype(BF16)
    sgate = _dot(hb, sg_ref[...])
    act = (sgate * _sigmoid(sgate) * _dot(hb, su_ref[...])).astype(BF16)
    base_ref[...] = x1 + mod[5:6] * _dot(act, sd_ref[...])

    h2_ref[...] = h2


def _merge(x2, o_rw, o_gm, gt, mod, seq, wr, wg, wo, g2n, router_w, router_b, sg, su, sd):
    n, d = x2.shape
    tm = MG_TM
    per_b = seq // tm
    ne = N_EXPERTS
    tile = lambda w: pl.BlockSpec((tm, w), lambda i: (i, 0))
    const2 = lambda a, b: pl.BlockSpec((a, b), lambda i: (0, 0))
    return pl.pallas_call(
        _merge_kernel,
        out_shape=(jax.ShapeDtypeStruct((n, d), F32),
                   jax.ShapeDtypeStruct((n, d), F32),
                   jax.ShapeDtypeStruct((n, LANES), I32),
                   jax.ShapeDtypeStruct((n, LANES), I32),
                   jax.ShapeDtypeStruct((n, LANES), F32),
                   jax.ShapeDtypeStruct((8, ne), F32)),
        grid=(n // tm,),
        in_specs=[tile(d), tile(RW_WIDTH), tile(GM_WIDTH), tile(2 * d),
                  pl.BlockSpec((1, 6, d), lambda i: (i // per_b, 0, 0)),
                  const2(RW_WIDTH, d), const2(GM_WIDTH, d), const2(d, d), const2(1, d),
                  const2(d, ne), const2(1, ne),
                  const2(d, SHARED_DIM), const2(d, SHARED_DIM), const2(SHARED_DIM, d)],
        out_specs=(tile(d), tile(d), tile(LANES), tile(LANES), tile(LANES),
                   pl.BlockSpec((8, ne), lambda i: (0, 0))),
        compiler_params=_params(("arbitrary",)),
    )(x2, o_rw, o_gm, gt, mod, wr, wg, wo, g2n.reshape(1, d), router_w, router_b.reshape(1, ne),
      sg, su, sd)


def _destmap_kernel(idx_ref, pos_ref, ps_ref, o_ref):
    tm = idx_ref.shape[0]
    idx = idx_ref[...]
    ps = ps_ref[...]
    lane_e = lax.broadcasted_iota(I32, (tm, N_EXPERTS), 1)
    lane_o = lax.broadcasted_iota(I32, (tm, LANES), 1)
    start = jnp.zeros((tm, LANES), F32)
    for kslot in range(TOP_K):
        hit = lane_e == idx[:, kslot:kslot + 1]
        s_k = jnp.sum(jnp.where(hit, ps, 0.0), axis=-1, keepdims=True)
        start = jnp.where(lane_o == kslot, s_k, start)
    o_ref[...] = pos_ref[...] + start.astype(I32)


def _destmap(idx, pos, pstarts_f32):
    n = idx.shape[0]
    tm = 1024
    return pl.pallas_call(
        _destmap_kernel,
        out_shape=jax.ShapeDtypeStruct((n, LANES), I32),
        grid=(n // tm,),
        in_specs=[pl.BlockSpec((tm, LANES), lambda i: (i, 0)),
                  pl.BlockSpec((tm, LANES), lambda i: (i, 0)),
                  pl.BlockSpec((1, N_EXPERTS), lambda i: (0, 0))],
        out_specs=pl.BlockSpec((tm, LANES), lambda i: (i, 0)),
        compiler_params=_params(("arbitrary",)),
    )(idx, pos, pstarts_f32.reshape(1, N_EXPERTS))


def _dispatch_kernel(zf_ref, dest_ref, h_ref, xs_ref, zbuf, sem):
    tm = DSP_TM
    bm = EXP_BM

    @pl.when(pl.program_id(0) == 0)
    def _():
        zbuf[...] = jnp.zeros_like(zbuf)

        def zero_copy(e):
            return pltpu.make_async_copy(zbuf, xs_ref.at[pl.ds(pl.multiple_of(zf_ref[e], bm), bm)], sem)

        def zstart(e, carry):
            @pl.when(zf_ref[e] >= 0)
            def _():
                zero_copy(e).start()
            return carry

        def zwait(e, carry):
            @pl.when(zf_ref[e] >= 0)
            def _():
                zero_copy(e).wait()
            return carry

        lax.fori_loop(0, 2 * N_EXPERTS, zstart, 0)
        lax.fori_loop(0, 2 * N_EXPERTS, zwait, 0)

    def issue(tok, carry):
        for kslot in range(TOP_K):
            pltpu.make_async_copy(h_ref.at[pl.ds(tok, 1)],
                                  xs_ref.at[pl.ds(dest_ref[tok * TOP_K + kslot], 1)],
                                  sem).start(priority=kslot % 2)
        return carry

    lax.fori_loop(0, tm, issue, 0, unroll=4)
    for kslot in range(TOP_K):
        pltpu.make_async_copy(h_ref, xs_ref.at[pl.ds(0, tm)], sem).wait()


def _dispatch(zfill, dest_flat, h2f, p_rows):
    n, w = h2f.shape
    tm = DSP_TM
    grid_spec = pltpu.PrefetchScalarGridSpec(
        num_scalar_prefetch=1,
        grid=(n // tm,),
        in_specs=[pl.BlockSpec((tm * TOP_K,), lambda i, zf: (i,), memory_space=pltpu.SMEM),
                  pl.BlockSpec((tm, w), lambda i, zf: (i, 0))],
        out_specs=pl.BlockSpec(memory_space=pl.ANY),
        scratch_shapes=[pltpu.VMEM((EXP_BM, w), F32), pltpu.SemaphoreType.DMA(())],
    )
    return pl.pallas_call(
        _dispatch_kernel,
        out_shape=jax.ShapeDtypeStruct((p_rows, w), F32),
        grid_spec=grid_spec,
        compiler_params=_params(("arbitrary",)),
    )(zfill, dest_flat, h2f)


def _experts_kernel(bstart_ref, bcnt_ref, nb_ref, wg_ref, wu_ref, wd_ref, xs_ref, o_ref,
                    xbuf, obuf, wgb, wub, wdb, in_sems, out_sems):
    e = pl.program_id(0)
    bm = EXP_BM
    nb = nb_ref[0]
    g0 = bstart_ref[e]
    nblk = o_ref.shape[0] // bm

    def rows(g):
        return pl.ds(pl.multiple_of(g * bm, bm), bm)

    def in_copy(g, slot):
        return pltpu.make_async_copy(xs_ref.at[rows(g)], xbuf.at[slot], in_sems.at[slot])

    def out_copy(g, slot):
        return pltpu.make_async_copy(obuf.at[slot], o_ref.at[rows(g)], out_sems.at[slot])

    @pl.when((e == 0) & (nb > 0))
    def _():
        in_copy(0, 0).start()

    @pl.when(bcnt_ref[e] > 0)
    def _():
        wgb[...] = wg_ref[0].astype(BF16)
        wub[...] = wu_ref[0].astype(BF16)
        wdb[...] = wd_ref[0].astype(BF16)

    def block(j, carry):
        g = g0 + j
        slot = lax.rem(g, 2)

        @pl.when(g + 1 < nb)
        def _():
            in_copy(g + 1, 1 - slot).start()

        in_copy(g, slot).wait()
        xb = xbuf[slot].astype(BF16)
        gate = _dot(xb, wgb[...])
        up = _dot(xb, wub[...])
        act = (gate * _sigmoid(gate) * up).astype(BF16)
        res = _dot(act, wdb[...])

        @pl.when(g >= 2)
        def _():
            out_copy(g - 2, slot).wait()

        obuf[slot] = res
        out_copy(g, slot).start()
        return carry

    lax.fori_loop(0, bcnt_ref[e], block, 0)

    @pl.when(e == pl.num_programs(0) - 1)
    def _():
        @pl.when(nb >= 2)
        def _():
            out_copy(nb - 2, lax.rem(nb, 2)).wait()

        @pl.when(nb >= 1)
        def _():
            out_copy(nb - 1, lax.rem(nb + 1, 2)).wait()

        obuf[0] = jnp.zeros(obuf.shape[1:], obuf.dtype)

        def zstart(g, carry):
            out_copy(g, 0).start()
            return carry

        def zwait(g, carry):
            out_copy(g, 0).wait()
            return carry

        lax.fori_loop(nb, nblk, zstart, 0)
        lax.fori_loop(nb, nblk, zwait, 0)


def _experts(bstart, bcnt, nblk_used, xs, w_gate, w_up, w_down):
    p_rows, d = xs.shape
    bm = EXP_BM
    wspec = lambda shape: pl.BlockSpec(shape, lambda e, bs, bc, nb: (e, 0, 0))
    grid_spec = pltpu.PrefetchScalarGridSpec(
        num_scalar_prefetch=3,
        grid=(N_EXPERTS,),
        in_specs=[wspec((1, d, EXPERT_DIM)), wspec((1, d, EXPERT_DIM)), wspec((1, EXPERT_DIM, d)),
                  pl.BlockSpec(memory_space=pl.ANY)],
        out_specs=pl.BlockSpec(memory_space=pl.ANY),
        scratch_shapes=[pltpu.VMEM((2, bm, d), F32),
                        pltpu.VMEM((2, bm, d), F32),
                        pltpu.VMEM((d, EXPERT_DIM), BF16),
                        pltpu.VMEM((d, EXPERT_DIM), BF16),
                        pltpu.VMEM((EXPERT_DIM, d), BF16),
                        pltpu.SemaphoreType.DMA((2,)),
                        pltpu.SemaphoreType.DMA((2,))],
    )
    return pl.pallas_call(
        _experts_kernel,
        out_shape=jax.ShapeDtypeStruct((p_rows, d), F32),
        grid_spec=grid_spec,
        compiler_params=_params(("arbitrary",)),
    )(bstart, bcnt, nblk_used, w_gate, w_up, w_down, xs)


def _combine_kernel(dcur_ref, dnxt_ref, wts_ref, base_ref, mod_ref, fg_ref, ob_ref, o_ref, buf, sems):
    tm = CMB_TM
    i = pl.program_id(0)
    slot = lax.rem(i, 2)

    def issue(d_ref, s):
        def body(tok, carry):
            for kslot in range(TOP_K):
                pltpu.make_async_copy(ob_ref.at[pl.ds(d_ref[tok * TOP_K + kslot], 1)],
                                      buf.at[s, kslot, pl.ds(tok, 1)],
                                      sems.at[s]).start(priority=kslot % 2)
            return carry
        lax.fori_loop(0, tm, body, 0, unroll=4)

    @pl.when(i == 0)
    def _():
        issue(dcur_ref, 0)

    @pl.when(i + 1 < pl.num_programs(0))
    def _():
        issue(dnxt_ref, 1 - slot)

    for kslot in range(TOP_K):
        pltpu.make_async_copy(ob_ref.at[pl.ds(0, tm)], buf.at[slot, kslot], sems.at[slot]).wait()

    wts = wts_ref[...]
    acc = wts[:, 0:1] * buf[slot, 0]
    for kslot in range(1, TOP_K):
        acc = acc + wts[:, kslot:kslot + 1] * buf[slot, kslot]
    x2 = base_ref[...] + mod_ref[0][5:6] * acc
    o_ref[...] = _rmsnorm_rows(x2, fg_ref[...])


def _combine(dest_flat, wts, base, mod, seq, final_g, ob):
    n, d = base.shape
    tm = CMB_TM
    per_b = seq // tm
    nt = n // tm
    return pl.pallas_call(
        _combine_kernel,
        out_shape=jax.ShapeDtypeStruct((n, d), F32),
        grid=(nt,),
        in_specs=[pl.BlockSpec((tm * TOP_K,), lambda i: (i,), memory_space=pltpu.SMEM),
                  pl.BlockSpec((tm * TOP_K,), lambda i: (jnp.minimum(i + 1, nt - 1),),
                               memory_space=pltpu.SMEM),
                  pl.BlockSpec((tm, LANES), lambda i: (i, 0)),
                  pl.BlockSpec((tm, d), lambda i: (i, 0)),
                  pl.BlockSpec((1, 6, d), lambda i: (i // per_b, 0, 0)),
                  pl.BlockSpec((1, d), lambda i: (0, 0)),
                  pl.BlockSpec(memory_space=pl.ANY)],
        out_specs=pl.BlockSpec((tm, d), lambda i: (i, 0)),
        scratch_shapes=[pltpu.VMEM((2, TOP_K, tm, d), F32), pltpu.SemaphoreType.DMA((2,))],
        compiler_params=_params(("arbitrary",)),
    )(dest_flat, dest_flat, wts, base, mod, final_g.reshape(1, d), ob)


def _layer(x2, bsz, seq, c, w_ada, b_ada, norm1_g, w_in, rw_mu, rw_w0, rw_w2, rw_a0, rw_a2, rw_g2,
           rw_k_k, rw_k_a, rw_r_k, rw_lnx_g, rw_lnx_b, gm_ln_g, gm_ln_b, gm_w_s, gm_b_s,
           w_br_rwkv, w_br_gmlp, w_out, norm2_g, router_w, router_b, moe_w_gate, moe_w_up,
           moe_w_down, sh_w_gate, sh_w_up, sh_w_down, final_g):
    n, d = x2.shape
    mod = _ada(c, w_ada, b_ada).reshape(bsz, 6, d)
    p_rw, gm, gt = _inproj(x2, mod, norm1_g, w_in.astype(BF16), seq)
    o_rw = _rwkv(p_rw, bsz, seq, rw_mu, rw_w0, rw_w2, rw_a0, rw_a2, rw_g2, rw_k_k, rw_k_a,
                 rw_r_k, rw_lnx_g, rw_lnx_b)
    o_gm = _gmlp(gm, gm_ln_g, gm_ln_b, gm_w_s, gm_b_s)
    base, h2f, idx, pos, wts, cnt = _merge(
        x2, o_rw, o_gm, gt, mod, seq, w_br_rwkv.astype(BF16), w_br_gmlp.astype(BF16),
        w_out.astype(BF16), norm2_g, router_w, router_b, sh_w_gate.astype(BF16),
        sh_w_up.astype(BF16), sh_w_down.astype(BF16))

    bm = EXP_BM
    counts = cnt[0].astype(I32)
    padded = (counts + bm - 1) // bm * bm
    pends = jnp.cumsum(padded)
    pstarts = pends - padded
    p_rows = n * TOP_K + N_EXPERTS * bm
    nblk = p_rows // bm
    dest = _destmap(idx, pos, pstarts.astype(F32))[:, :TOP_K].reshape(-1)
    nblk_used = (pends[-1:] // bm).astype(I32)
    tail = nblk_used[0] + jnp.arange(N_EXPERTS, dtype=I32)
    zfill = jnp.concatenate([jnp.where(padded > 0, pends - bm, -1),
                             jnp.where(tail < nblk, tail * bm, -1)]).astype(I32)

    xs = _dispatch(zfill, dest, h2f, p_rows)
    ob = _experts((pstarts // bm).astype(I32), (padded // bm).astype(I32), nblk_used, xs,
                  moe_w_gate, moe_w_up, moe_w_down)
    return _combine(dest, wts, base, mod, seq, final_g, ob)


def kernel(x, c, w_ada, b_ada, norm1_g, w_in, rw_mu, rw_w0, rw_w2, rw_a0, rw_a2, rw_g2, rw_k_k, rw_k_a, rw_r_k, rw_lnx_g, rw_lnx_b, gm_ln_g, gm_ln_b, gm_w_s, gm_b_s, w_br_rwkv, w_br_gmlp, w_out, norm2_g, router_w, router_b, moe_w_gate, moe_w_up, moe_w_down, sh_w_gate, sh_w_up, sh_w_down, final_g):
    bsz, seq, d = x.shape
    assert d == D_MODEL and w_ada.shape[0] == 1, "single-layer block of width D_MODEL"
    assert seq % IN_TM == 0 and seq % RW_BLOCK == 0 and seq % GM_TM == 0 and seq % MG_TM == 0
    assert bsz % RW_NB == 0
    out = _layer(x.reshape(bsz * seq, d), bsz, seq, c, w_ada[0], b_ada[0], norm1_g[0], w_in[0],
                 rw_mu[0], rw_w0[0], rw_w2[0], rw_a0[0], rw_a2[0], rw_g2[0], rw_k_k[0], rw_k_a[0],
                 rw_r_k[0].reshape(-1), rw_lnx_g[0], rw_lnx_b[0], gm_ln_g[0].reshape(-1),
                 gm_ln_b[0].reshape(-1), gm_w_s[0], gm_b_s[0], w_br_rwkv[0], w_br_gmlp[0],
                 w_out[0], norm2_g[0], router_w[0], router_b[0], moe_w_gate[0], moe_w_up[0],
                 moe_w_down[0], sh_w_gate[0], sh_w_up[0], sh_w_down[0], final_g)
    return out.reshape(bsz, seq, d)
```

```python
import functools

import jax
import jax.numpy as jnp
from jax import lax
from jax.experimental import pallas as pl
from jax.experimental.pallas import tpu as pltpu

F32 = jnp.float32
BF16 = jnp.bfloat16
I32 = jnp.int32

D_MODEL = 1024
RW_WIDTH = 512
RW_HEADS = 8
RW_HEAD_DIM = 64
RW_DECAY_LORA = 64
RW_AAA_LORA = 64
RW_GATE_LORA = 128
RW_COLS = 3 * RW_WIDTH + RW_DECAY_LORA + RW_AAA_LORA + RW_GATE_LORA
RW_LNX_EPS = 64e-5
GM_WIDTH = 512
GM_GROUP_DIM = 64
GM_GROUPS = 8
GM_CHUNK = 128
N_EXPERTS = 256
TOP_K = 8
EXPERT_DIM = 256
SHARED_DIM = 256
ROUTED_SCALE = 2.5
NORM_EPS = 1e-6
LN_EPS = 1e-5

LANES = 128
SUBLANES = 8
VMEM_LIMIT = 56 * 1024 * 1024

WKV_CHUNK = 64
WKV_QUAD = 4
QW = WKV_QUAD * RW_HEAD_DIM
RW_BLOCK = 256
RW_NB = 2
IN_TM = 512
GM_TM = 512
MG_TM = 256
EXP_BM = 256
EXP_IN_SLOTS = 3
DSP_TM = 1024
CMB_TM = 128


def _dot(a, b):
    return jnp.dot(a, b, preferred_element_type=F32)


def _dot_nt(a, b):
    return lax.dot_general(a, b, (((1,), (1,)), ((), ())), preferred_element_type=F32)


def _dot_tn(a, b):
    return lax.dot_general(a, b, (((0,), (0,)), ((), ())), preferred_element_type=F32)


def _split2(a):
    hi = a.astype(BF16)
    lo = (a - hi.astype(F32)).astype(BF16)
    return hi, lo


def _split3(a):
    h1 = a.astype(BF16)
    r1 = a - h1.astype(F32)
    h2 = r1.astype(BF16)
    h3 = (r1 - h2.astype(F32)).astype(BF16)
    return h1, h2, h3


def _dot_hp(a, b):
    ah, al = _split2(a)
    bh, bl = _split2(b)
    return _dot(ah, bh) + _dot(al, bh) + _dot(ah, bl)


def _dot_exact_rhs(a, b_bf16):
    h1, h2, h3 = _split3(a)
    return _dot(h1, b_bf16) + _dot(h2, b_bf16) + _dot(h3, b_bf16)


def _sigmoid(x):
    return 1.0 / (1.0 + jnp.exp(-x))


def _rmsnorm_rows(x, g):
    return x * lax.rsqrt(jnp.mean(x * x, axis=-1, keepdims=True) + NORM_EPS) * g


def _params(sem):
    return pltpu.CompilerParams(dimension_semantics=sem, vmem_limit_bytes=VMEM_LIMIT)


def _ada_kernel(c_ref, w_ref, b_ref, o_ref):
    c = c_ref[...]
    s = c * _sigmoid(c)
    o_ref[...] = _dot_hp(s, w_ref[...]) + b_ref[...]


def _ada(c, w, b):
    bsz, d = c.shape
    n = w.shape[1]
    tn = 1024
    return pl.pallas_call(
        _ada_kernel,
        out_shape=jax.ShapeDtypeStruct((bsz, n), F32),
        grid=(n // tn,),
        in_specs=[pl.BlockSpec((bsz, d), lambda j: (0, 0)),
                  pl.BlockSpec((d, tn), lambda j: (0, j)),
                  pl.BlockSpec((1, tn), lambda j: (0, j))],
        out_specs=pl.BlockSpec((bsz, tn), lambda j: (0, j)),
        compiler_params=_params(("arbitrary",)),
    )(c, w, b.reshape(1, n))


def _inproj_kernel(x_ref, mod_ref, g_ref, w_ref, prw_ref, gm_ref, gt_ref):
    mod = mod_ref[0]
    h = _rmsnorm_rows(x_ref[...], g_ref[...]) * (1.0 + mod[1:2]) + mod[0:1]
    hb = h.astype(BF16)
    prw_ref[...] = _dot(hb, w_ref[:, :RW_COLS])
    pgm = _dot(hb, w_ref[:, RW_COLS:RW_COLS + 2 * GM_WIDTH])
    gm_ref[...] = jax.nn.gelu(pgm, approximate=True).astype(BF16)
    pgt = _dot(hb, w_ref[:, RW_COLS + 2 * GM_WIDTH:])
    gt_ref[...] = _sigmoid(pgt).astype(BF16)


def _inproj(x2, mod, g1, w_in_bf, seq):
    n, d = x2.shape
    tm = IN_TM
    per_b = seq // tm
    cols = w_in_bf.shape[1]
    return pl.pallas_call(
        _inproj_kernel,
        out_shape=(jax.ShapeDtypeStruct((n, RW_COLS), F32),
                   jax.ShapeDtypeStruct((n, 2 * GM_WIDTH), BF16),
                   jax.ShapeDtypeStruct((n, 2 * d), BF16)),
        grid=(n // tm,),
        in_specs=[pl.BlockSpec((tm, d), lambda i: (i, 0)),
                  pl.BlockSpec((1, 6, d), lambda i: (i // per_b, 0, 0)),
                  pl.BlockSpec((1, d), lambda i: (0, 0)),
                  pl.BlockSpec((d, cols), lambda i: (0, 0), pipeline_mode=pl.Buffered(1))],
        out_specs=(pl.BlockSpec((tm, RW_COLS), lambda i: (i, 0)),
                   pl.BlockSpec((tm, 2 * GM_WIDTH), lambda i: (i, 0)),
                   pl.BlockSpec((tm, 2 * d), lambda i: (i, 0))),
        compiler_params=_params(("arbitrary",)),
    )(x2, mod, g1.reshape(1, d), w_in_bf)


def _rwkv_kernel(p_ref, mu_ref, w0_ref, w2_ref, a0_ref, a2_ref, g2_ref, kk_ref, ka_ref,
                 rk_ref, lg_ref, lb_ref, e_ref, o_ref, prev_ref, s_ref):
    nb = RW_NB
    tb = RW_BLOCK
    t = WKV_CHUNK
    c = RW_WIDTH
    nq = RW_HEADS // WKV_QUAD
    nch = tb // t

    @pl.when(pl.program_id(1) == 0)
    def _():
        prev_ref[...] = jnp.zeros_like(prev_ref)
        s_ref[...] = jnp.zeros_like(s_ref)

    p = p_ref[...].reshape(nb * tb, RW_COLS)
    row = lax.broadcasted_iota(I32, p.shape, 0)
    prev = pltpu.roll(p, 1, 0)
    for bi in range(nb):
        prev = jnp.where(row == bi * tb, prev_ref[bi:bi + 1, :], prev)
        prev_ref[bi:bi + 1, :] = p[(bi + 1) * tb - 1:(bi + 1) * tb, :]
    xs = p + (prev - p) * mu_ref[...]
    r = xs[:, 0:c]
    k = xs[:, c:2 * c]
    v = xs[:, 2 * c:3 * c]
    xwa = xs[:, 3 * c:3 * c + LANES]
    xg = xs[:, 3 * c + LANES:3 * c + 2 * LANES]

    wl = w0_ref[...] + _dot_hp(jnp.tanh(xwa), w2_ref[...])
    nz = -wl
    w_log = -(jnp.maximum(nz, 0.0) + jnp.log(1.0 + jnp.exp(-jnp.abs(nz)))) - 0.5
    lw = -jnp.exp(w_log)
    a_sig = _sigmoid(a0_ref[...] + _dot_hp(xwa, a2_ref[...]))
    g = _dot(_sigmoid(xg).astype(BF16), g2_ref[...].astype(BF16))

    e = e_ref[...]

    def headsum(z):
        hi, lo = _split2(z)
        return _dot(hi, e) + _dot(lo, e)

    kk = k * kk_ref[...]
    kk = kk * lax.rsqrt(jnp.maximum(headsum(kk * kk), 1e-24))
    k2 = k * (1.0 + (a_sig - 1.0) * ka_ref[...])
    a_v = -kk
    b_v = kk * a_sig

    ri = lax.broadcasted_iota(I32, (tb, tb), 0)
    ci = lax.broadcasted_iota(I32, (tb, tb), 1)
    tri = jnp.where((ri // t == ci // t) & (ri >= ci), 1.0, 0.0).astype(BF16)
    cum = jnp.concatenate([_dot_exact_lhs(tri, lw[bi * tb:(bi + 1) * tb, :]) for bi in range(nb)], axis=0)
    ends = [cum[(j + 1) * t - 1:(j + 1) * t, :] for j in range(nb * nch)]
    cum_end = jnp.concatenate([jnp.broadcast_to(z, (t, c)) for z in ends], axis=0)

    w_abs = jnp.exp(cum)
    at = (a_v * jnp.exp(cum - lw)).astype(BF16)
    rt = (r * w_abs).astype(BF16)
    w_inv = jnp.exp(-cum)
    kh = (k2 * w_inv).astype(BF16)
    bh = (b_v * w_inv).astype(BF16)
    vb = v.astype(BF16)
    w_rem = jnp.exp(cum_end - cum)
    kw = (k2 * w_rem).astype(BF16)
    bw = (b_v * w_rem).astype(BF16)

    lane = lax.broadcasted_iota(I32, (1, QW), 1)
    hmask = [jnp.where(lane // RW_HEAD_DIM == h, 1.0, 0.0).astype(BF16) for h in range(WKV_QUAD)]

    def stack_heads(z):
        return jnp.concatenate([z * m for m in hmask], axis=0)

    tt = lax.broadcasted_iota(I32, (t, QW), 0)
    ss = lax.broadcasted_iota(I32, (t, QW), 1) % t
    m_strict = tt > ss
    m_incl = tt >= ss
    eye = jnp.where(tt == ss, 1.0, 0.0)
    r4 = lax.broadcasted_iota(I32, (QW, QW), 0)
    c4 = lax.broadcasted_iota(I32, (QW, QW), 1)
    bdm = jnp.where((r4 // t) == (c4 // t), 1.0, 0.0)

    keys = [(bi, ch, q) for bi in range(nb) for ch in range(nch) for q in range(nq)]
    sl = {(bi, ch, q): (slice(bi * tb + ch * t, bi * tb + (ch + 1) * t), slice(q * QW, (q + 1) * QW))
          for bi, ch, q in keys}
    ar = {key: jnp.concatenate([at[sl[key]], rt[sl[key]]], axis=0) for key in keys}
    aa = {key: _dot_nt(ar[key], jnp.concatenate([stack_heads(kh[sl[key]]), stack_heads(bh[sl[key]])],
                                                 axis=0))
          for key in keys}
    a_ak = {key: jnp.where(m_strict, aa[key][0:t, 0:QW], 0.0).astype(BF16) for key in keys}
    a_ab = {key: jnp.where(m_strict, aa[key][0:t, QW:2 * QW], 0.0) for key in keys}
    a_rk = {key: jnp.where(m_incl, aa[key][t:2 * t, 0:QW], 0.0).astype(BF16) for key in keys}
    a_rb = {key: jnp.where(m_incl, aa[key][t:2 * t, QW:2 * QW], 0.0).astype(BF16) for key in keys}
    tinv = {key: eye + a_ab[key] for key in keys}
    pw = {}
    for key in keys:
        lb = a_ab[key].astype(BF16)
        pw[key] = _dot(lb, stack_heads(lb)).astype(BF16)
    for _ in range(4):
        both = {key: _dot(jnp.concatenate([pw[key], tinv[key].astype(BF16)], axis=0), stack_heads(pw[key]))
                for key in keys}
        tinv = {key: tinv[key] + both[key][t:2 * t] for key in keys}
        pw = {key: both[key][0:t].astype(BF16) for key in keys}
    tinv = {key: (tinv[key] + _dot(tinv[key].astype(BF16), stack_heads(pw[key]))).astype(BF16)
            for key in keys}
    avv = {key: _dot(jnp.concatenate([a_ak[key], a_rk[key]], axis=0), stack_heads(vb[sl[key]]))
           for key in keys}

    y_blk = {}
    for ch in range(nch):
        ks = [(bi, ch, q) for bi in range(nb) for q in range(nq)]
        s_q = {key: s_ref[key[0] * nq + key[2]] for key in ks}
        sst = {key: _dot_nt(ar[key], s_q[key].astype(BF16)) for key in ks}
        ub = {key: _dot(tinv[key], stack_heads((sst[key][0:t] + avv[key][0:t]).astype(BF16))).astype(BF16)
              for key in ks}
        upd = {key: _dot_tn(jnp.concatenate([vb[sl[key]], ub[key]], axis=0),
                            jnp.concatenate([kw[sl[key]], bw[sl[key]]], axis=0)) for key in ks}
        for key in ks:
            rs, cs = sl[key]
            w_tot = jnp.exp(ends[key[0] * nch + ch][:, cs])
            s_ref[key[0] * nq + key[2]] = s_q[key] * w_tot + upd[key] * bdm
        for key in ks:
            y_blk[key] = sst[key][t:2 * t] + avv[key][t:2 * t] + _dot(a_rb[key], stack_heads(ub[key]))
    y = jnp.concatenate([jnp.concatenate([y_blk[bi, ch, q] for q in range(nq)], axis=1)
                         for bi in range(nb) for ch in range(nch)], axis=0)

    inv_n = 1.0 / RW_HEAD_DIM
    m = headsum(y) * inv_n
    dlt = y - m
    var = headsum(dlt * dlt) * inv_n
    yn = dlt * lax.rsqrt(var + RW_LNX_EPS) * lg_ref[...] + lb_ref[...]
    bonus = headsum(r * k2 * rk_ref[...]) * v
    o_ref[...] = ((yn + bonus) * g).astype(BF16).reshape(nb, tb, c)


def _dot_exact_lhs(tri_bf16, a):
    h1, h2, h3 = _split3(a)
    return _dot(tri_bf16, h1) + _dot(tri_bf16, h2) + _dot(tri_bf16, h3)


def _rwkv(p_rw, bsz, seq, mu, w0, w2, a0, a2, g2, k_k, k_a, r_k, lnx_g, lnx_b):
    c = RW_WIDTH
    tb = RW_BLOCK
    nb = RW_NB
    zeros = jnp.zeros((RW_DECAY_LORA, c), F32)
    w2p = jnp.concatenate([w2, zeros], axis=0)
    a2p = jnp.concatenate([zeros, a2], axis=0)
    hid = jnp.arange(c, dtype=I32) // RW_HEAD_DIM
    e = (hid[:, None] == hid[None, :]).astype(BF16)
    row = lambda z: z.reshape(1, -1)
    const = lambda shape: pl.BlockSpec(shape, lambda b, j: (0,) * len(shape))
    out = pl.pallas_call(
        _rwkv_kernel,
        out_shape=jax.ShapeDtypeStruct((bsz, seq, c), BF16),
        grid=(bsz // nb, seq // tb),
        in_specs=[pl.BlockSpec((nb, tb, RW_COLS), lambda b, j: (b, j, 0)),
                  const((1, RW_COLS)), const((1, c)), const((LANES, c)), const((1, c)),
                  const((LANES, c)), const((RW_GATE_LORA, c)), const((1, c)), const((1, c)),
                  const((1, c)), const((1, c)), const((1, c)), const((c, c))],
        out_specs=pl.BlockSpec((nb, tb, c), lambda b, j: (b, j, 0)),
        scratch_shapes=[pltpu.VMEM((nb, RW_COLS), F32),
                        pltpu.VMEM((nb * (RW_HEADS // WKV_QUAD), QW, QW), F32)],
        compiler_params=_params(("arbitrary", "arbitrary")),
    )(p_rw.reshape(bsz, seq, RW_COLS), row(mu), row(w0), w2p, row(a0), a2p, g2, row(k_k), row(k_a),
      row(r_k), row(lnx_g), row(lnx_b), e)
    return out.reshape(bsz * seq, c)


def _gmlp_kernel(z_ref, lg_ref, lb_ref, ws_ref, bst_ref, e_ref, o_ref):
    w = GM_WIDTH
    ch = GM_CHUNK
    z = z_ref[...].astype(F32)
    u = z[:, :w]
    v = z[:, w:]
    e = e_ref[...]
    inv_n = 1.0 / GM_GROUP_DIM

    def groupsum(x):
        hi, lo = _split2(x)
        return _dot(hi, e) + _dot(lo, e)

    m = groupsum(v) * inv_n
    dlt = v - m
    var = groupsum(dlt * dlt) * inv_n
    vn = (dlt * lax.rsqrt(var + LN_EPS) * lg_ref[...] + lb_ref[...]).astype(BF16)

    ri = lax.broadcasted_iota(I32, (ch, ch), 0)
    ci = lax.broadcasted_iota(I32, (ch, ch), 1)
    low = ri >= ci
    lane = lax.broadcasted_iota(I32, (1, LANES), 1)
    m_lo = jnp.where(lane < GM_GROUP_DIM, 1.0, 0.0).astype(BF16)
    m_hi = jnp.where(lane >= GM_GROUP_DIM, 1.0, 0.0).astype(BF16)
    bst = bst_ref[...]
    for pr in range(GM_GROUPS // 2):
        g0, g1 = 2 * pr, 2 * pr + 1
        wcat = jnp.concatenate([jnp.where(low, ws_ref[g0], 0.0), jnp.where(low, ws_ref[g1], 0.0)],
                               axis=1).astype(BF16)
        bias = jnp.where(lane < GM_GROUP_DIM, bst[:, g0:g0 + 1], bst[:, g1:g1 + 1])
        ls = slice(pr * LANES, (pr + 1) * LANES)
        for cc in range(z.shape[0] // ch):
            rs = slice(cc * ch, (cc + 1) * ch)
            vp = vn[rs, ls]
            rhs = jnp.concatenate([vp * m_lo, vp * m_hi], axis=0)
            sv = _dot(wcat, rhs) + bias
            o_ref[rs, ls] = (u[rs, ls] * sv).astype(BF16)


def _gmlp(gm, ln_g, ln_b, w_s, b_s):
    n = gm.shape[0]
    w = GM_WIDTH
    tm = GM_TM
    gid = jnp.arange(w, dtype=I32) // GM_GROUP_DIM
    e = (gid[:, None] == gid[None, :]).astype(BF16)
    return pl.pallas_call(
        _gmlp_kernel,
        out_shape=jax.ShapeDtypeStruct((n, w), BF16),
        grid=(n // tm,),
        in_specs=[pl.BlockSpec((tm, 2 * w), lambda i: (i, 0)),
                  pl.BlockSpec((1, w), lambda i: (0, 0)),
                  pl.BlockSpec((1, w), lambda i: (0, 0)),
                  pl.BlockSpec((GM_GROUPS, GM_CHUNK, GM_CHUNK), lambda i: (0, 0, 0)),
                  pl.BlockSpec((GM_CHUNK, GM_GROUPS), lambda i: (0, 0)),
                  pl.BlockSpec((w, w), lambda i: (0, 0))],
        out_specs=pl.BlockSpec((tm, w), lambda i: (i, 0)),
        compiler_params=_params(("arbitrary",)),
    )(gm, ln_g.reshape(1, w), ln_b.reshape(1, w), w_s, b_s.T, e)


def _merge_kernel(x_ref, orw_ref, ogm_ref, gt_ref, mod_ref, wr_ref, wg_ref, wo_ref, g2_ref,
                  rw_ref, rb_ref, sg_ref, su_ref, sd_ref,
                  base_ref, h2_ref, idx_ref, pos_ref, wts_ref, cnt_ref):
    d = D_MODEL
    tm = MG_TM
    ne = N_EXPERTS

    @pl.when(pl.program_id(0) == 0)
    def _():
        cnt_ref[...] = jnp.zeros_like(cnt_ref)

    mod = mod_ref[0]
    gt = gt_ref[...]
    br = _dot(orw_ref[...], wr_ref[...])
    bg = _dot(ogm_ref[...], wg_ref[...])
    merged = gt[:, :d].astype(F32) * br + gt[:, d:].astype(F32) * bg
    x1 = x_ref[...] + mod[2:3] * _dot(merged.astype(BF16), wo_ref[...])
    h2 = _rmsnorm_rows(x1, g2_ref[...]) * (1.0 + mod[4:5]) + mod[3:4]

    h2_ref[...] = h2
    logits = _dot_hp(h2, rw_ref[...])

    hb = h2.astype(BF16)
    sgate = _dot(hb, sg_ref[...])
    act = (sgate * _sigmoid(sgate) * _dot(hb, su_ref[...])).astype(BF16)
    base_ref[...] = x1 + mod[5:6] * _dot(act, sd_ref[...])

    scores = _sigmoid(logits)
    cur = scores + rb_ref[...]
    lane_e = lax.broadcasted_iota(I32, (tm, ne), 1).astype(F32)
    ri = lax.broadcasted_iota(I32, (tm, tm), 0)
    ci = lax.broadcasted_iota(I32, (tm, tm), 1)
    tri = jnp.where(ri > ci, 1.0, 0.0).astype(BF16)
    onehots, sels, ixs = [], [], []
    for _ in range(TOP_K):
        mx = jnp.max(cur, axis=-1, keepdims=True)
        ix = jnp.min(jnp.where(cur == mx, lane_e, float(ne)), axis=-1, keepdims=True)
        oh = lane_e == ix
        sels.append(jnp.sum(jnp.where(oh, scores, 0.0), axis=-1, keepdims=True))
        cur = jnp.where(oh, -jnp.inf, cur)
        onehots.append(oh)
        ixs.append(ix)
    chosen = onehots[0]
    for oh in onehots[1:]:
        chosen = chosen | oh
    cmask = jnp.where(chosen, 1.0, 0.0)
    denom = sels[0]
    for s in sels[1:]:
        denom = denom + s
    scale = ROUTED_SCALE / denom
    rank = cnt_ref[0:1, :] + _dot(tri, cmask.astype(BF16))
    cnt_ref[0:1, :] = cnt_ref[0:1, :] + jnp.sum(cmask, axis=0, keepdims=True)
    lane_o = lax.broadcasted_iota(I32, (tm, LANES), 1)
    idx_o = jnp.zeros((tm, LANES), F32)
    pos_o = jnp.zeros((tm, LANES), F32)
    wts_o = jnp.zeros((tm, LANES), F32)
    for kslot in range(TOP_K):
        pos_k = jnp.sum(jnp.where(onehots[kslot], rank, 0.0), axis=-1, keepdims=True)
        here = lane_o == kslot
        idx_o = jnp.where(here, ixs[kslot], idx_o)
        pos_o = jnp.where(here, pos_k, pos_o)
        wts_o = jnp.where(here, sels[kslot] * scale, wts_o)
    idx_ref[...] = idx_o.astype(I32)
    pos_ref[...] = pos_o.astype(I32)
    wts_ref[...] = wts_o


def _merge(x2, o_rw, o_gm, gt, mod, seq, wr, wg, wo, g2n, router_w, router_b, sg, su, sd):
    n, d = x2.shape
    tm = MG_TM
    per_b = seq // tm
    ne = N_EXPERTS
    tile = lambda w: pl.BlockSpec((tm, w), lambda i: (i, 0))
    const2 = lambda a, b: pl.BlockSpec((a, b), lambda i: (0, 0))
    return pl.pallas_call(
        _merge_kernel,
        out_shape=(jax.ShapeDtypeStruct((n, d), F32),
                   jax.ShapeDtypeStruct((n, d), F32),
                   jax.ShapeDtypeStruct((n, LANES), I32),
                   jax.ShapeDtypeStruct((n, LANES), I32),
                   jax.ShapeDtypeStruct((n, LANES), F32),
                   jax.ShapeDtypeStruct((8, ne), F32)),
        grid=(n // tm,),
        in_specs=[tile(d), tile(RW_WIDTH), tile(GM_WIDTH), tile(2 * d),
                  pl.BlockSpec((1, 6, d), lambda i: (i // per_b, 0, 0)),
                  const2(RW_WIDTH, d), const2(GM_WIDTH, d), const2(d, d), const2(1, d),
                  const2(d, ne), const2(1, ne),
                  const2(d, SHARED_DIM), const2(d, SHARED_DIM), const2(SHARED_DIM, d)],
        out_specs=(tile(d), tile(d), tile(LANES), tile(LANES), tile(LANES),
                   pl.BlockSpec((8, ne), lambda i: (0, 0))),
        compiler_params=_params(("arbitrary",)),
    )(x2, o_rw, o_gm, gt, mod, wr, wg, wo, g2n.reshape(1, d), router_w, router_b.reshape(1, ne),
      sg, su, sd)


def _destmap_kernel(idx_ref, pos_ref, ps_ref, o_ref):
    tm = idx_ref.shape[0]
    idx = idx_ref[...]
    ps = ps_ref[...]
    lane_e = lax.broadcasted_iota(I32, (tm, N_EXPERTS), 1)
    lane_o = lax.broadcasted_iota(I32, (tm, LANES), 1)
    start = jnp.zeros((tm, LANES), F32)
    for kslot in range(TOP_K):
        hit = lane_e == idx[:, kslot:kslot + 1]
        s_k = jnp.sum(jnp.where(hit, ps, 0.0), axis=-1, keepdims=True)
        start = jnp.where(lane_o == kslot, s_k, start)
    o_ref[...] = pos_ref[...] + start.astype(I32)


def _destmap(idx, pos, pstarts_f32):
    n = idx.shape[0]
    tm = 1024
    return pl.pallas_call(
        _destmap_kernel,
        out_shape=jax.ShapeDtypeStruct((n, LANES), I32),
        grid=(n // tm,),
        in_specs=[pl.BlockSpec((tm, LANES), lambda i: (i, 0)),
                  pl.BlockSpec((tm, LANES), lambda i: (i, 0)),
                  pl.BlockSpec((1, N_EXPERTS), lambda i: (0, 0))],
        out_specs=pl.BlockSpec((tm, LANES), lambda i: (i, 0)),
        compiler_params=_params(("arbitrary",)),
    )(idx, pos, pstarts_f32.reshape(1, N_EXPERTS))


def _dispatch_kernel(zf_ref, dest_ref, h_ref, xs_ref, zbuf, sem):
    tm = DSP_TM
    bm = EXP_BM

    @pl.when(pl.program_id(0) == 0)
    def _():
        zbuf[...] = jnp.zeros_like(zbuf)

        def zero_copy(e):
            return pltpu.make_async_copy(zbuf, xs_ref.at[pl.ds(pl.multiple_of(zf_ref[e], bm), bm)], sem)

        def zstart(e, carry):
            @pl.when(zf_ref[e] >= 0)
            def _():
                zero_copy(e).start()
            return carry

        def zwait(e, carry):
            @pl.when(zf_ref[e] >= 0)
            def _():
                zero_copy(e).wait()
            return carry

        lax.fori_loop(0, 2 * N_EXPERTS, zstart, 0)
        lax.fori_loop(0, 2 * N_EXPERTS, zwait, 0)

    def issue(grp, carry):
        tok0 = pl.multiple_of(grp * SUBLANES, SUBLANES)
        for u in range(SUBLANES):
            for kslot in range(TOP_K):
                pltpu.make_async_copy(h_ref.at[pl.ds(tok0 + u, 1)],
                                      xs_ref.at[pl.ds(dest_ref[(tok0 + u) * TOP_K + kslot], 1)],
                                      sem).start(priority=kslot % 2)
        return carry

    lax.fori_loop(0, tm // SUBLANES, issue, 0)
    for kslot in range(TOP_K):
        pltpu.make_async_copy(h_ref, xs_ref.at[pl.ds(0, tm)], sem).wait()


def _dispatch(zfill, dest_flat, h2f, p_rows):
    n, w = h2f.shape
    tm = DSP_TM
    grid_spec = pltpu.PrefetchScalarGridSpec(
        num_scalar_prefetch=1,
        grid=(n // tm,),
        in_specs=[pl.BlockSpec((tm * TOP_K,), lambda i, zf: (i,), memory_space=pltpu.SMEM),
                  pl.BlockSpec((tm, w), lambda i, zf: (i, 0))],
        out_specs=pl.BlockSpec(memory_space=pl.ANY),
        scratch_shapes=[pltpu.VMEM((EXP_BM, w), F32), pltpu.SemaphoreType.DMA(())],
    )
    return pl.pallas_call(
        _dispatch_kernel,
        out_shape=jax.ShapeDtypeStruct((p_rows, w), F32),
        grid_spec=grid_spec,
        compiler_params=_params(("arbitrary",)),
    )(zfill, dest_flat, h2f)


def _experts_kernel(bstart_ref, bcnt_ref, nb_ref, wg_ref, wu_ref, wd_ref, xs_ref, o_ref,
                    xbuf, obuf, wgb, wub, wdb, in_sems, out_sems):
    e = pl.program_id(0)
    bm = EXP_BM
    nb = nb_ref[0]
    g0 = bstart_ref[e]
    nblk = o_ref.shape[0] // bm

    def rows(g):
        return pl.ds(pl.multiple_of(g * bm, bm), bm)

    def in_copy(g, slot):
        return pltpu.make_async_copy(xs_ref.at[rows(g)], xbuf.at[slot], in_sems.at[slot])

    def out_copy(g, slot):
        return pltpu.make_async_copy(obuf.at[slot], o_ref.at[rows(g)], out_sems.at[slot])

    ahead = EXP_IN_SLOTS - 1
    for g_first in range(ahead):
        @pl.when((e == 0) & (nb > g_first))
        def _():
            in_copy(g_first, g_first).start()

    @pl.when(bcnt_ref[e] > 0)
    def _():
        wgb[...] = wg_ref[0].astype(BF16)
        wub[...] = wu_ref[0].astype(BF16)
        wdb[...] = wd_ref[0].astype(BF16)

    def block(j, carry):
        g = g0 + j
        slot = lax.rem(g, 2)
        islot = lax.rem(g, EXP_IN_SLOTS)

        @pl.when(g + ahead < nb)
        def _():
            in_copy(g + ahead, lax.rem(g + ahead, EXP_IN_SLOTS)).start()

        in_copy(g, islot).wait()
        xb = xbuf[islot].astype(BF16)
        gate = _dot(xb, wgb[...])
        up = _dot(xb, wub[...])
        act = (gate * _sigmoid(gate) * up).astype(BF16)
        res = _dot(act, wdb[...])

        @pl.when(g >= 2)
        def _():
            out_copy(g - 2, slot).wait()

        obuf[slot] = res
        out_copy(g, slot).start()
        return carry

    lax.fori_loop(0, bcnt_ref[e], block, 0)

    @pl.when(e == pl.num_programs(0) - 1)
    def _():
        @pl.when(nb >= 2)
        def _():
            out_copy(nb - 2, lax.rem(nb, 2)).wait()

        @pl.when(nb >= 1)
        def _():
            out_copy(nb - 1, lax.rem(nb + 1, 2)).wait()

        obuf[0] = jnp.zeros(obuf.shape[1:], obuf.dtype)

        def zstart(g, carry):
            out_copy(g, 0).start()
            return carry

        def zwait(g, carry):
            out_copy(g, 0).wait()
            return carry

        lax.fori_loop(nb, nblk, zstart, 0)
        lax.fori_loop(nb, nblk, zwait, 0)


def _experts(bstart, bcnt, nblk_used, xs, w_gate, w_up, w_down):
    p_rows, d = xs.shape
    bm = EXP_BM
    wspec = lambda shape: pl.BlockSpec(shape, lambda e, bs, bc, nb: (e, 0, 0))
    grid_spec = pltpu.PrefetchScalarGridSpec(
        num_scalar_prefetch=3,
        grid=(N_EXPERTS,),
        in_specs=[wspec((1, d, EXPERT_DIM)), wspec((1, d, EXPERT_DIM)), wspec((1, EXPERT_DIM, d)),
                  pl.BlockSpec(memory_space=pl.ANY)],
        out_specs=pl.BlockSpec(memory_space=pl.ANY),
        scratch_shapes=[pltpu.VMEM((EXP_IN_SLOTS, bm, d), F32),
                        pltpu.VMEM((2, bm, d), F32),
                        pltpu.VMEM((d, EXPERT_DIM), BF16),
                        pltpu.VMEM((d, EXPERT_DIM), BF16),
                        pltpu.VMEM((EXPERT_DIM, d), BF16),
                        pltpu.SemaphoreType.DMA((EXP_IN_SLOTS,)),
                        pltpu.SemaphoreType.DMA((2,))],
    )
    return pl.pallas_call(
        _experts_kernel,
        out_shape=jax.ShapeDtypeStruct((p_rows, d), F32),
        grid_spec=grid_spec,
        compiler_params=_params(("arbitrary",)),
    )(bstart, bcnt, nblk_used, w_gate, w_up, w_down, xs)


def _combine_kernel(dcur_ref, dnxt_ref, wts_ref, base_ref, mod_ref, fg_ref, ob_ref, o_ref, buf, sems):
    tm = CMB_TM
    i = pl.program_id(0)
    slot = lax.rem(i, 2)

    def issue(d_ref, s):
        def body(grp, carry):
            tok0 = pl.multiple_of(grp * SUBLANES, SUBLANES)
            for u in range(SUBLANES):
                for kslot in range(TOP_K):
                    pltpu.make_async_copy(ob_ref.at[pl.ds(d_ref[(tok0 + u) * TOP_K + kslot], 1)],
                                          buf.at[s, kslot, pl.ds(tok0 + u, 1)],
                                          sems.at[s]).start(priority=kslot % 2)
            return carry
        lax.fori_loop(0, tm // SUBLANES, body, 0)

    @pl.when(i == 0)
    def _():
        issue(dcur_ref, 0)

    for s in range(2):
        @pl.when((i + 1 < pl.num_programs(0)) & (slot == 1 - s))
        def _():
            issue(dnxt_ref, s)

    for kslot in range(TOP_K):
        pltpu.make_async_copy(ob_ref.at[pl.ds(0, tm)], buf.at[slot, kslot], sems.at[slot]).wait()

    wts = wts_ref[...]
    acc = wts[:, 0:1] * buf[slot, 0]
    for kslot in range(1, TOP_K):
        acc = acc + wts[:, kslot:kslot + 1] * buf[slot, kslot]
    x2 = base_ref[...] + mod_ref[0][5:6] * acc
    o_ref[...] = _rmsnorm_rows(x2, fg_ref[...])


def _combine(dest_flat, wts, base, mod, seq, final_g, ob):
    n, d = base.shape
    tm = CMB_TM
    per_b = seq // tm
    nt = n // tm
    return pl.pallas_call(
        _combine_kernel,
        out_shape=jax.ShapeDtypeStruct((n, d), F32),
        grid=(nt,),
        in_specs=[pl.BlockSpec((tm * TOP_K,), lambda i: (i,), memory_space=pltpu.SMEM),
                  pl.BlockSpec((tm * TOP_K,), lambda i: (jnp.minimum(i + 1, nt - 1),),
                               memory_space=pltpu.SMEM),
                  pl.BlockSpec((tm, LANES), lambda i: (i, 0)),
                  pl.BlockSpec((tm, d), lambda i: (i, 0)),
                  pl.BlockSpec((1, 6, d), lambda i: (i // per_b, 0, 0)),
                  pl.BlockSpec((1, d), lambda i: (0, 0)),
                  pl.BlockSpec(memory_space=pl.ANY)],
        out_specs=pl.BlockSpec((tm, d), lambda i: (i, 0)),
        scratch_shapes=[pltpu.VMEM((2, TOP_K, tm, d), F32), pltpu.SemaphoreType.DMA((2,))],
        compiler_params=_params(("arbitrary",)),
    )(dest_flat, dest_flat, wts, base, mod, final_g.reshape(1, d), ob)


def _layer(x2, bsz, seq, c, w_ada, b_ada, norm1_g, w_in, rw_mu, rw_w0, rw_w2, rw_a0, rw_a2, rw_g2,
           rw_k_k, rw_k_a, rw_r_k, rw_lnx_g, rw_lnx_b, gm_ln_g, gm_ln_b, gm_w_s, gm_b_s,
           w_br_rwkv, w_br_gmlp, w_out, norm2_g, router_w, router_b, moe_w_gate, moe_w_up,
           moe_w_down, sh_w_gate, sh_w_up, sh_w_down, final_g):
    n, d = x2.shape
    mod = _ada(c, w_ada, b_ada).reshape(bsz, 6, d)
    p_rw, gm, gt = _inproj(x2, mod, norm1_g, w_in.astype(BF16), seq)
    o_rw = _rwkv(p_rw, bsz, seq, rw_mu, rw_w0, rw_w2, rw_a0, rw_a2, rw_g2, rw_k_k, rw_k_a,
                 rw_r_k, rw_lnx_g, rw_lnx_b)
    o_gm = _gmlp(gm, gm_ln_g, gm_ln_b, gm_w_s, gm_b_s)
    base, h2f, idx, pos, wts, cnt = _merge(
        x2, o_rw, o_gm, gt, mod, seq, w_br_rwkv.astype(BF16), w_br_gmlp.astype(BF16),
        w_out.astype(BF16), norm2_g, router_w, router_b, sh_w_gate.astype(BF16),
        sh_w_up.astype(BF16), sh_w_down.astype(BF16))

    bm = EXP_BM
    counts = cnt[0].astype(I32)
    padded = (counts + bm - 1) // bm * bm
    pends = jnp.cumsum(padded)
    pstarts = pends - padded
    p_rows = n * TOP_K + N_EXPERTS * bm
    nblk = p_rows // bm
    dest = _destmap(idx, pos, pstarts.astype(F32))[:, :TOP_K].reshape(-1)
    nblk_used = (pends[-1:] // bm).astype(I32)
    tail = nblk_used[0] + jnp.arange(N_EXPERTS, dtype=I32)
    zfill = jnp.concatenate([jnp.where(padded > 0, pends - bm, -1),
                             jnp.where(tail < nblk, tail * bm, -1)]).astype(I32)

    xs = _dispatch(zfill, dest, h2f, p_rows)
    ob = _experts((pstarts // bm).astype(I32), (padded // bm).astype(I32), nblk_used, xs,
                  moe_w_gate, moe_w_up, moe_w_down)
    return _combine(dest, wts, base, mod, seq, final_g, ob)


def kernel(x, c, w_ada, b_ada, norm1_g, w_in, rw_mu, rw_w0, rw_w2, rw_a0, rw_a2, rw_g2, rw_k_k, rw_k_a, rw_r_k, rw_lnx_g, rw_lnx_b, gm_ln_g, gm_ln_b, gm_w_s, gm_b_s, w_br_rwkv, w_br_gmlp, w_out, norm2_g, router_w, router_b, moe_w_gate, moe_w_up, moe_w_down, sh_w_gate, sh_w_up, sh_w_down, final_g):
    bsz, seq, d = x.shape
    assert d == D_MODEL and w_ada.shape[0] == 1, "single-layer block of width D_MODEL"
    assert seq % IN_TM == 0 and seq % RW_BLOCK == 0 and seq % GM_TM == 0 and seq % MG_TM == 0
    assert bsz % RW_NB == 0
    out = _layer(x.reshape(bsz * seq, d), bsz, seq, c, w_ada[0], b_ada[0], norm1_g[0], w_in[0],
                 rw_mu[0], rw_w0[0], rw_w2[0], rw_a0[0], rw_a2[0], rw_g2[0], rw_k_k[0], rw_k_a[0],
                 rw_r_k[0].reshape(-1), rw_lnx_g[0], rw_lnx_b[0], gm_ln_g[0].reshape(-1),
                 gm_ln_b[0].reshape(-1), gm_w_s[0], gm_b_s[0], w_br_rwkv[0], w_br_gmlp[0],
                 w_out[0], norm2_g[0], router_w[0], router_b[0], moe_w_gate[0], moe_w_up[0],
                 moe_w_down[0], sh_w_gate[0], sh_w_up[0], sh_w_down[0], final_g)
    return out.reshape(bsz, seq, d)
```

```python
import functools

import jax
import jax.numpy as jnp
from jax import lax
from jax.experimental import pallas as pl
from jax.experimental.pallas import tpu as pltpu

F32 = jnp.float32
BF16 = jnp.bfloat16
I32 = jnp.int32

D_MODEL = 1024
RW_WIDTH = 512
RW_HEADS = 8
RW_HEAD_DIM = 64
RW_DECAY_LORA = 64
RW_AAA_LORA = 64
RW_GATE_LORA = 128
RW_COLS = 3 * RW_WIDTH + RW_DECAY_LORA + RW_AAA_LORA + RW_GATE_LORA
RW_LNX_EPS = 64e-5
GM_WIDTH = 512
GM_GROUP_DIM = 64
GM_GROUPS = 8
GM_CHUNK = 128
N_EXPERTS = 256
TOP_K = 8
EXPERT_DIM = 256
SHARED_DIM = 256
ROUTED_SCALE = 2.5
NORM_EPS = 1e-6
LN_EPS = 1e-5

LANES = 128
SUBLANES = 8
ROW_TILE = D_MODEL // LANES
VMEM_LIMIT = 56 * 1024 * 1024

WKV_CHUNK = 64
WKV_QUAD = 4
QW = WKV_QUAD * RW_HEAD_DIM
RW_BLOCK = 256
RW_NB = 2
IN_TM = 512
GM_TM = 512
MG_TM = 256
EXP_BM = 256
EXP_IN_SLOTS = 4
DSP_TM = 1024
CMB_TM = 128


def _dot(a, b):
    return jnp.dot(a, b, preferred_element_type=F32)


def _dot_nt(a, b):
    return lax.dot_general(a, b, (((1,), (1,)), ((), ())), preferred_element_type=F32)


def _dot_tn(a, b):
    return lax.dot_general(a, b, (((0,), (0,)), ((), ())), preferred_element_type=F32)


def _split2(a):
    hi = a.astype(BF16)
    lo = (a - hi.astype(F32)).astype(BF16)
    return hi, lo


def _split3(a):
    h1 = a.astype(BF16)
    r1 = a - h1.astype(F32)
    h2 = r1.astype(BF16)
    h3 = (r1 - h2.astype(F32)).astype(BF16)
    return h1, h2, h3


def _dot_hp(a, b):
    ah, al = _split2(a)
    bh, bl = _split2(b)
    return _dot(ah, bh) + _dot(al, bh) + _dot(ah, bl)


def _dot_exact_rhs(a, b_bf16):
    h1, h2, h3 = _split3(a)
    return _dot(h1, b_bf16) + _dot(h2, b_bf16) + _dot(h3, b_bf16)


def _sigmoid(x):
    return 1.0 / (1.0 + jnp.exp(-x))


def _rmsnorm_rows(x, g):
    return x * lax.rsqrt(jnp.mean(x * x, axis=-1, keepdims=True) + NORM_EPS) * g


def _to_row_tiles(ref, x):
    m = x.shape[0]
    for c in range(ROW_TILE):
        ref[pl.ds(c, m, stride=ROW_TILE), :] = x[:, c * LANES:(c + 1) * LANES]


def _from_row_tiles(ref):
    m = ref.shape[0] // ROW_TILE
    return jnp.concatenate([ref[pl.ds(c, m, stride=ROW_TILE), :] for c in range(ROW_TILE)], axis=1)


def _params(sem):
    return pltpu.CompilerParams(dimension_semantics=sem, vmem_limit_bytes=VMEM_LIMIT)


def _ada_kernel(c_ref, w_ref, b_ref, o_ref):
    c = c_ref[...]
    s = c * _sigmoid(c)
    o_ref[...] = _dot_hp(s, w_ref[...]) + b_ref[...]


def _ada(c, w, b):
    bsz, d = c.shape
    n = w.shape[1]
    tn = 1024
    return pl.pallas_call(
        _ada_kernel,
        out_shape=jax.ShapeDtypeStruct((bsz, n), F32),
        grid=(n // tn,),
        in_specs=[pl.BlockSpec((bsz, d), lambda j: (0, 0)),
                  pl.BlockSpec((d, tn), lambda j: (0, j)),
                  pl.BlockSpec((1, tn), lambda j: (0, j))],
        out_specs=pl.BlockSpec((bsz, tn), lambda j: (0, j)),
        compiler_params=_params(("arbitrary",)),
    )(c, w, b.reshape(1, n))


def _inproj_kernel(x_ref, mod_ref, g_ref, w_ref, prw_ref, gm_ref, gt_ref):
    mod = mod_ref[0]
    h = _rmsnorm_rows(x_ref[...], g_ref[...]) * (1.0 + mod[1:2]) + mod[0:1]
    hb = h.astype(BF16)
    prw_ref[...] = _dot(hb, w_ref[:, :RW_COLS])
    pgm = _dot(hb, w_ref[:, RW_COLS:RW_COLS + 2 * GM_WIDTH])
    gm_ref[...] = jax.nn.gelu(pgm, approximate=True).astype(BF16)
    pgt = _dot(hb, w_ref[:, RW_COLS + 2 * GM_WIDTH:])
    gt_ref[...] = _sigmoid(pgt).astype(BF16)


def _inproj(x2, mod, g1, w_in_bf, seq):
    n, d = x2.shape
    tm = IN_TM
    per_b = seq // tm
    cols = w_in_bf.shape[1]
    return pl.pallas_call(
        _inproj_kernel,
        out_shape=(jax.ShapeDtypeStruct((n, RW_COLS), F32),
                   jax.ShapeDtypeStruct((n, 2 * GM_WIDTH), BF16),
                   jax.ShapeDtypeStruct((n, 2 * d), BF16)),
        grid=(n // tm,),
        in_specs=[pl.BlockSpec((tm, d), lambda i: (i, 0)),
                  pl.BlockSpec((1, 6, d), lambda i: (i // per_b, 0, 0)),
                  pl.BlockSpec((1, d), lambda i: (0, 0)),
                  pl.BlockSpec((d, cols), lambda i: (0, 0), pipeline_mode=pl.Buffered(1))],
        out_specs=(pl.BlockSpec((tm, RW_COLS), lambda i: (i, 0)),
                   pl.BlockSpec((tm, 2 * GM_WIDTH), lambda i: (i, 0)),
                   pl.BlockSpec((tm, 2 * d), lambda i: (i, 0))),
        compiler_params=_params(("arbitrary",)),
    )(x2, mod, g1.reshape(1, d), w_in_bf)


def _rwkv_kernel(p_ref, mu_ref, w0_ref, w2_ref, a0_ref, a2_ref, g2_ref, kk_ref, ka_ref,
                 rk_ref, lg_ref, lb_ref, e_ref, o_ref, prev_ref, s_ref):
    nb = RW_NB
    tb = RW_BLOCK
    t = WKV_CHUNK
    c = RW_WIDTH
    nq = RW_HEADS // WKV_QUAD
    nch = tb // t

    @pl.when(pl.program_id(1) == 0)
    def _():
        prev_ref[...] = jnp.zeros_like(prev_ref)
        s_ref[...] = jnp.zeros_like(s_ref)

    p = p_ref[...].reshape(nb * tb, RW_COLS)
    row = lax.broadcasted_iota(I32, p.shape, 0)
    prev = pltpu.roll(p, 1, 0)
    for bi in range(nb):
        prev = jnp.where(row == bi * tb, prev_ref[bi:bi + 1, :], prev)
        prev_ref[bi:bi + 1, :] = p[(bi + 1) * tb - 1:(bi + 1) * tb, :]
    xs = p + (prev - p) * mu_ref[...]
    r = xs[:, 0:c]
    k = xs[:, c:2 * c]
    v = xs[:, 2 * c:3 * c]
    xwa = xs[:, 3 * c:3 * c + LANES]
    xg = xs[:, 3 * c + LANES:3 * c + 2 * LANES]

    wl = w0_ref[...] + _dot_hp(jnp.tanh(xwa), w2_ref[...])
    nz = -wl
    w_log = -(jnp.maximum(nz, 0.0) + jnp.log(1.0 + jnp.exp(-jnp.abs(nz)))) - 0.5
    lw = -jnp.exp(w_log)
    a_sig = _sigmoid(a0_ref[...] + _dot_hp(xwa, a2_ref[...]))
    g = _dot(_sigmoid(xg).astype(BF16), g2_ref[...].astype(BF16))

    e = e_ref[...]

    def headsum(z):
        hi, lo = _split2(z)
        return _dot(hi, e) + _dot(lo, e)

    kk = k * kk_ref[...]
    kk = kk * lax.rsqrt(jnp.maximum(headsum(kk * kk), 1e-24))
    k2 = k * (1.0 + (a_sig - 1.0) * ka_ref[...])
    a_v = -kk
    b_v = kk * a_sig

    ri = lax.broadcasted_iota(I32, (tb, tb), 0)
    ci = lax.broadcasted_iota(I32, (tb, tb), 1)
    tri = jnp.where((ri // t == ci // t) & (ri >= ci), 1.0, 0.0).astype(BF16)
    cum = jnp.concatenate([_dot_exact_lhs(tri, lw[bi * tb:(bi + 1) * tb, :]) for bi in range(nb)], axis=0)
    ends = [cum[(j + 1) * t - 1:(j + 1) * t, :] for j in range(nb * nch)]
    cum_end = jnp.concatenate([jnp.broadcast_to(z, (t, c)) for z in ends], axis=0)

    w_abs = jnp.exp(cum)
    at = (a_v * jnp.exp(cum - lw)).astype(BF16)
    rt = (r * w_abs).astype(BF16)
    w_inv = jnp.exp(-cum)
    kh = (k2 * w_inv).astype(BF16)
    bh = (b_v * w_inv).astype(BF16)
    vb = v.astype(BF16)
    w_rem = jnp.exp(cum_end - cum)
    kw = (k2 * w_rem).astype(BF16)
    bw = (b_v * w_rem).astype(BF16)

    lane = lax.broadcasted_iota(I32, (1, QW), 1)
    hmask = [jnp.where(lane // RW_HEAD_DIM == h, 1.0, 0.0).astype(BF16) for h in range(WKV_QUAD)]

    def stack_heads(z):
        return jnp.concatenate([z * m for m in hmask], axis=0)

    tt = lax.broadcasted_iota(I32, (t, QW), 0)
    ss = lax.broadcasted_iota(I32, (t, QW), 1) % t
    m_strict = tt > ss
    m_incl = tt >= ss
    eye = jnp.where(tt == ss, 1.0, 0.0)
    r4 = lax.broadcasted_iota(I32, (QW, QW), 0)
    c4 = lax.broadcasted_iota(I32, (QW, QW), 1)
    bdm = jnp.where((r4 // t) == (c4 // t), 1.0, 0.0)

    keys = [(bi, ch, q) for bi in range(nb) for ch in range(nch) for q in range(nq)]
    sl = {(bi, ch, q): (slice(bi * tb + ch * t, bi * tb + (ch + 1) * t), slice(q * QW, (q + 1) * QW))
          for bi, ch, q in keys}
    ar = {key: jnp.concatenate([at[sl[key]], rt[sl[key]]], axis=0) for key in keys}
    aa = {key: _dot_nt(ar[key], jnp.concatenate([stack_heads(kh[sl[key]]), stack_heads(bh[sl[key]])],
                                                 axis=0))
          for key in keys}
    a_ak = {key: jnp.where(m_strict, aa[key][0:t, 0:QW], 0.0).astype(BF16) for key in keys}
    a_ab = {key: jnp.where(m_strict, aa[key][0:t, QW:2 * QW], 0.0) for key in keys}
    a_rk = {key: jnp.where(m_incl, aa[key][t:2 * t, 0:QW], 0.0).astype(BF16) for key in keys}
    a_rb = {key: jnp.where(m_incl, aa[key][t:2 * t, QW:2 * QW], 0.0).astype(BF16) for key in keys}
    tinv = {key: eye + a_ab[key] for key in keys}
    pw = {}
    for key in keys:
        lb = a_ab[key].astype(BF16)
        pw[key] = _dot(lb, stack_heads(lb)).astype(BF16)
    for _ in range(4):
        both = {key: _dot(jnp.concatenate([pw[key], tinv[key].astype(BF16)], axis=0), stack_heads(pw[key]))
                for key in keys}
        tinv = {key: tinv[key] + both[key][t:2 * t] for key in keys}
        pw = {key: both[key][0:t].astype(BF16) for key in keys}
    tinv = {key: (tinv[key] + _dot(tinv[key].astype(BF16), stack_heads(pw[key]))).astype(BF16)
            for key in keys}
    avv = {key: _dot(jnp.concatenate([a_ak[key], a_rk[key]], axis=0), stack_heads(vb[sl[key]]))
           for key in keys}

    y_blk = {}
    for ch in range(nch):
        ks = [(bi, ch, q) for bi in range(nb) for q in range(nq)]
        s_q = {key: s_ref[key[0] * nq + key[2]] for key in ks}
        sst = {key: _dot_nt(ar[key], s_q[key].astype(BF16)) for key in ks}
        ub = {key: _dot(tinv[key], stack_heads((sst[key][0:t] + avv[key][0:t]).astype(BF16))).astype(BF16)
              for key in ks}
        upd = {key: _dot_tn(jnp.concatenate([vb[sl[key]], ub[key]], axis=0),
                            jnp.concatenate([kw[sl[key]], bw[sl[key]]], axis=0)) for key in ks}
        for key in ks:
            rs, cs = sl[key]
            w_tot = jnp.exp(ends[key[0] * nch + ch][:, cs])
            s_ref[key[0] * nq + key[2]] = s_q[key] * w_tot + upd[key] * bdm
        for key in ks:
            y_blk[key] = sst[key][t:2 * t] + avv[key][t:2 * t] + _dot(a_rb[key], stack_heads(ub[key]))
    y = jnp.concatenate([jnp.concatenate([y_blk[bi, ch, q] for q in range(nq)], axis=1)
                         for bi in range(nb) for ch in range(nch)], axis=0)

    inv_n = 1.0 / RW_HEAD_DIM
    m = headsum(y) * inv_n
    dlt = y - m
    var = headsum(dlt * dlt) * inv_n
    yn = dlt * lax.rsqrt(var + RW_LNX_EPS) * lg_ref[...] + lb_ref[...]
    bonus = headsum(r * k2 * rk_ref[...]) * v
    o_ref[...] = ((yn + bonus) * g).astype(BF16).reshape(nb, tb, c)


def _dot_exact_lhs(tri_bf16, a):
    h1, h2, h3 = _split3(a)
    return _dot(tri_bf16, h1) + _dot(tri_bf16, h2) + _dot(tri_bf16, h3)


def _rwkv(p_rw, bsz, seq, mu, w0, w2, a0, a2, g2, k_k, k_a, r_k, lnx_g, lnx_b):
    c = RW_WIDTH
    tb = RW_BLOCK
    nb = RW_NB
    zeros = jnp.zeros((RW_DECAY_LORA, c), F32)
    w2p = jnp.concatenate([w2, zeros], axis=0)
    a2p = jnp.concatenate([zeros, a2], axis=0)
    hid = jnp.arange(c, dtype=I32) // RW_HEAD_DIM
    e = (hid[:, None] == hid[None, :]).astype(BF16)
    row = lambda z: z.reshape(1, -1)
    const = lambda shape: pl.BlockSpec(shape, lambda b, j: (0,) * len(shape))
    out = pl.pallas_call(
        _rwkv_kernel,
        out_shape=jax.ShapeDtypeStruct((bsz, seq, c), BF16),
        grid=(bsz // nb, seq // tb),
        in_specs=[pl.BlockSpec((nb, tb, RW_COLS), lambda b, j: (b, j, 0)),
                  const((1, RW_COLS)), const((1, c)), const((LANES, c)), const((1, c)),
                  const((LANES, c)), const((RW_GATE_LORA, c)), const((1, c)), const((1, c)),
                  const((1, c)), const((1, c)), const((1, c)), const((c, c))],
        out_specs=pl.BlockSpec((nb, tb, c), lambda b, j: (b, j, 0)),
        scratch_shapes=[pltpu.VMEM((nb, RW_COLS), F32),
                        pltpu.VMEM((nb * (RW_HEADS // WKV_QUAD), QW, QW), F32)],
        compiler_params=_params(("arbitrary", "arbitrary")),
    )(p_rw.reshape(bsz, seq, RW_COLS), row(mu), row(w0), w2p, row(a0), a2p, g2, row(k_k), row(k_a),
      row(r_k), row(lnx_g), row(lnx_b), e)
    return out.reshape(bsz * seq, c)


def _gmlp_kernel(z_ref, lg_ref, lb_ref, ws_ref, bst_ref, e_ref, o_ref):
    w = GM_WIDTH
    ch = GM_CHUNK
    z = z_ref[...].astype(F32)
    u = z[:, :w]
    v = z[:, w:]
    e = e_ref[...]
    inv_n = 1.0 / GM_GROUP_DIM

    def groupsum(x):
        hi, lo = _split2(x)
        return _dot(hi, e) + _dot(lo, e)

    m = groupsum(v) * inv_n
    dlt = v - m
    var = groupsum(dlt * dlt) * inv_n
    vn = (dlt * lax.rsqrt(var + LN_EPS) * lg_ref[...] + lb_ref[...]).astype(BF16)

    ri = lax.broadcasted_iota(I32, (ch, ch), 0)
    ci = lax.broadcasted_iota(I32, (ch, ch), 1)
    low = ri >= ci
    lane = lax.broadcasted_iota(I32, (1, LANES), 1)
    m_lo = jnp.where(lane < GM_GROUP_DIM, 1.0, 0.0).astype(BF16)
    m_hi = jnp.where(lane >= GM_GROUP_DIM, 1.0, 0.0).astype(BF16)
    bst = bst_ref[...]
    for pr in range(GM_GROUPS // 2):
        g0, g1 = 2 * pr, 2 * pr + 1
        wcat = jnp.concatenate([jnp.where(low, ws_ref[g0], 0.0), jnp.where(low, ws_ref[g1], 0.0)],
                               axis=1).astype(BF16)
        bias = jnp.where(lane < GM_GROUP_DIM, bst[:, g0:g0 + 1], bst[:, g1:g1 + 1])
        ls = slice(pr * LANES, (pr + 1) * LANES)
        for cc in range(z.shape[0] // ch):
            rs = slice(cc * ch, (cc + 1) * ch)
            vp = vn[rs, ls]
            rhs = jnp.concatenate([vp * m_lo, vp * m_hi], axis=0)
            sv = _dot(wcat, rhs) + bias
            o_ref[rs, ls] = (u[rs, ls] * sv).astype(BF16)


def _gmlp(gm, ln_g, ln_b, w_s, b_s):
    n = gm.shape[0]
    w = GM_WIDTH
    tm = GM_TM
    gid = jnp.arange(w, dtype=I32) // GM_GROUP_DIM
    e = (gid[:, None] == gid[None, :]).astype(BF16)
    return pl.pallas_call(
        _gmlp_kernel,
        out_shape=jax.ShapeDtypeStruct((n, w), BF16),
        grid=(n // tm,),
        in_specs=[pl.BlockSpec((tm, 2 * w), lambda i: (i, 0)),
                  pl.BlockSpec((1, w), lambda i: (0, 0)),
                  pl.BlockSpec((1, w), lambda i: (0, 0)),
                  pl.BlockSpec((GM_GROUPS, GM_CHUNK, GM_CHUNK), lambda i: (0, 0, 0)),
                  pl.BlockSpec((GM_CHUNK, GM_GROUPS), lambda i: (0, 0)),
                  pl.BlockSpec((w, w), lambda i: (0, 0))],
        out_specs=pl.BlockSpec((tm, w), lambda i: (i, 0)),
        compiler_params=_params(("arbitrary",)),
    )(gm, ln_g.reshape(1, w), ln_b.reshape(1, w), w_s, b_s.T, e)


def _merge_kernel(x_ref, orw_ref, ogm_ref, gt_ref, mod_ref, wr_ref, wg_ref, wo_ref, g2_ref,
                  rw_ref, rb_ref, sg_ref, su_ref, sd_ref,
                  base_ref, h2_ref, idx_ref, pos_ref, wts_ref, cnt_ref):
    d = D_MODEL
    tm = MG_TM
    ne = N_EXPERTS

    @pl.when(pl.program_id(0) == 0)
    def _():
        cnt_ref[...] = jnp.zeros_like(cnt_ref)

    mod = mod_ref[0]
    gt = gt_ref[...]
    br = _dot(orw_ref[...], wr_ref[...])
    bg = _dot(ogm_ref[...], wg_ref[...])
    merged = gt[:, :d].astype(F32) * br + gt[:, d:].astype(F32) * bg
    x1 = x_ref[...] + mod[2:3] * _dot(merged.astype(BF16), wo_ref[...])
    h2 = _rmsnorm_rows(x1, g2_ref[...]) * (1.0 + mod[4:5]) + mod[3:4]

    _to_row_tiles(h2_ref, h2)
    logits = _dot_hp(h2, rw_ref[...])

    hb = h2.astype(BF16)
    sgate = _dot(hb, sg_ref[...])
    act = (sgate * _sigmoid(sgate) * _dot(hb, su_ref[...])).astype(BF16)
    base_ref[...] = x1 + mod[5:6] * _dot(act, sd_ref[...])

    scores = _sigmoid(logits)
    cur = scores + rb_ref[...]
    lane_e = lax.broadcasted_iota(I32, (tm, ne), 1).astype(F32)
    ri = lax.broadcasted_iota(I32, (tm, tm), 0)
    ci = lax.broadcasted_iota(I32, (tm, tm), 1)
    tri = jnp.where(ri > ci, 1.0, 0.0).astype(BF16)
    onehots, sels, ixs = [], [], []
    for _ in range(TOP_K):
        mx = jnp.max(cur, axis=-1, keepdims=True)
        ix = jnp.min(jnp.where(cur == mx, lane_e, float(ne)), axis=-1, keepdims=True)
        oh = lane_e == ix
        sels.append(jnp.sum(jnp.where(oh, scores, 0.0), axis=-1, keepdims=True))
        cur = jnp.where(oh, -jnp.inf, cur)
        onehots.append(oh)
        ixs.append(ix)
    chosen = onehots[0]
    for oh in onehots[1:]:
        chosen = chosen | oh
    cmask = jnp.where(chosen, 1.0, 0.0)
    denom = sels[0]
    for s in sels[1:]:
        denom = denom + s
    scale = ROUTED_SCALE / denom
    rank = cnt_ref[0:1, :] + _dot(tri, cmask.astype(BF16))
    cnt_ref[0:1, :] = cnt_ref[0:1, :] + jnp.sum(cmask, axis=0, keepdims=True)
    lane_o = lax.broadcasted_iota(I32, (tm, LANES), 1)
    idx_o = jnp.zeros((tm, LANES), F32)
    pos_o = jnp.zeros((tm, LANES), F32)
    wts_o = jnp.zeros((tm, LANES), F32)
    for kslot in range(TOP_K):
        pos_k = jnp.sum(jnp.where(onehots[kslot], rank, 0.0), axis=-1, keepdims=True)
        here = lane_o == kslot
        idx_o = jnp.where(here, ixs[kslot], idx_o)
        pos_o = jnp.where(here, pos_k, pos_o)
        wts_o = jnp.where(here, sels[kslot] * scale, wts_o)
    idx_ref[...] = idx_o.astype(I32)
    pos_ref[...] = pos_o.astype(I32)
    wts_ref[...] = wts_o


def _merge(x2, o_rw, o_gm, gt, mod, seq, wr, wg, wo, g2n, router_w, router_b, sg, su, sd):
    n, d = x2.shape
    tm = MG_TM
    per_b = seq // tm
    ne = N_EXPERTS
    tile = lambda w: pl.BlockSpec((tm, w), lambda i: (i, 0))
    const2 = lambda a, b: pl.BlockSpec((a, b), lambda i: (0, 0))
    return pl.pallas_call(
        _merge_kernel,
        out_shape=(jax.ShapeDtypeStruct((n, d), F32),
                   jax.ShapeDtypeStruct((n * ROW_TILE, LANES), F32),
                   jax.ShapeDtypeStruct((n, LANES), I32),
                   jax.ShapeDtypeStruct((n, LANES), I32),
                   jax.ShapeDtypeStruct((n, LANES), F32),
                   jax.ShapeDtypeStruct((8, ne), F32)),
        grid=(n // tm,),
        in_specs=[tile(d), tile(RW_WIDTH), tile(GM_WIDTH), tile(2 * d),
                  pl.BlockSpec((1, 6, d), lambda i: (i // per_b, 0, 0)),
                  const2(RW_WIDTH, d), const2(GM_WIDTH, d), const2(d, d), const2(1, d),
                  const2(d, ne), const2(1, ne),
                  const2(d, SHARED_DIM), const2(d, SHARED_DIM), const2(SHARED_DIM, d)],
        out_specs=(tile(d), pl.BlockSpec((tm * ROW_TILE, LANES), lambda i: (i, 0)),
                   tile(LANES), tile(LANES), tile(LANES),
                   pl.BlockSpec((8, ne), lambda i: (0, 0))),
        compiler_params=_params(("arbitrary",)),
    )(x2, o_rw, o_gm, gt, mod, wr, wg, wo, g2n.reshape(1, d), router_w, router_b.reshape(1, ne),
      sg, su, sd)


def _destmap_kernel(idx_ref, pos_ref, ps_ref, o_ref):
    tm = idx_ref.shape[0]
    idx = idx_ref[...]
    ps = ps_ref[...]
    lane_e = lax.broadcasted_iota(I32, (tm, N_EXPERTS), 1)
    lane_o = lax.broadcasted_iota(I32, (tm, LANES), 1)
    start = jnp.zeros((tm, LANES), F32)
    for kslot in range(TOP_K):
        hit = lane_e == idx[:, kslot:kslot + 1]
        s_k = jnp.sum(jnp.where(hit, ps, 0.0), axis=-1, keepdims=True)
        start = jnp.where(lane_o == kslot, s_k, start)
    o_ref[...] = pos_ref[...] + start.astype(I32)


def _destmap(idx, pos, pstarts_f32):
    n = idx.shape[0]
    tm = 1024
    return pl.pallas_call(
        _destmap_kernel,
        out_shape=jax.ShapeDtypeStruct((n, LANES), I32),
        grid=(n // tm,),
        in_specs=[pl.BlockSpec((tm, LANES), lambda i: (i, 0)),
                  pl.BlockSpec((tm, LANES), lambda i: (i, 0)),
                  pl.BlockSpec((1, N_EXPERTS), lambda i: (0, 0))],
        out_specs=pl.BlockSpec((tm, LANES), lambda i: (i, 0)),
        compiler_params=_params(("arbitrary",)),
    )(idx, pos, pstarts_f32.reshape(1, N_EXPERTS))


def _dispatch_kernel(zf_ref, dest_ref, h_ref, xs_ref, zbuf, sem):
    tm = DSP_TM
    bm = EXP_BM

    @pl.when(pl.program_id(0) == 0)
    def _():
        zbuf[...] = jnp.zeros_like(zbuf)

        def zero_copy(e):
            first = pl.multiple_of(zf_ref[e] * ROW_TILE, bm * ROW_TILE)
            return pltpu.make_async_copy(zbuf, xs_ref.at[pl.ds(first, bm * ROW_TILE)], sem)

        def zstart(e, carry):
            @pl.when(zf_ref[e] >= 0)
            def _():
                zero_copy(e).start()
            return carry

        def zwait(e, carry):
            @pl.when(zf_ref[e] >= 0)
            def _():
                zero_copy(e).wait()
            return carry

        lax.fori_loop(0, 2 * N_EXPERTS, zstart, 0)
        lax.fori_loop(0, 2 * N_EXPERTS, zwait, 0)

    def issue(grp, carry):
        tok0 = pl.multiple_of(grp * SUBLANES, SUBLANES)
        for u in range(SUBLANES):
            for kslot in range(TOP_K):
                dst = pl.multiple_of(dest_ref[(tok0 + u) * TOP_K + kslot] * ROW_TILE, ROW_TILE)
                pltpu.make_async_copy(h_ref.at[pl.ds((tok0 + u) * ROW_TILE, ROW_TILE)],
                                      xs_ref.at[pl.ds(dst, ROW_TILE)],
                                      sem).start(priority=kslot % 2)
        return carry

    lax.fori_loop(0, tm // SUBLANES, issue, 0)
    for kslot in range(TOP_K):
        pltpu.make_async_copy(h_ref, xs_ref.at[pl.ds(0, tm * ROW_TILE)], sem).wait()


def _dispatch(zfill, dest_flat, h2f, p_rows):
    n = h2f.shape[0] // ROW_TILE
    tm = DSP_TM
    grid_spec = pltpu.PrefetchScalarGridSpec(
        num_scalar_prefetch=1,
        grid=(n // tm,),
        in_specs=[pl.BlockSpec((tm * TOP_K,), lambda i, zf: (i,), memory_space=pltpu.SMEM),
                  pl.BlockSpec((tm * ROW_TILE, LANES), lambda i, zf: (i, 0))],
        out_specs=pl.BlockSpec(memory_space=pl.ANY),
        scratch_shapes=[pltpu.VMEM((EXP_BM * ROW_TILE, LANES), F32), pltpu.SemaphoreType.DMA(())],
    )
    return pl.pallas_call(
        _dispatch_kernel,
        out_shape=jax.ShapeDtypeStruct((p_rows * ROW_TILE, LANES), F32),
        grid_spec=grid_spec,
        compiler_params=_params(("arbitrary",)),
    )(zfill, dest_flat, h2f)


def _experts_kernel(bstart_ref, bcnt_ref, nb_ref, wg_ref, wu_ref, wd_ref, xs_ref, o_ref,
                    xbuf, obuf, wgb, wub, wdb, in_sems, out_sems):
    e = pl.program_id(0)
    bm = EXP_BM
    nb = nb_ref[0]
    g0 = bstart_ref[e]
    nblk = o_ref.shape[0] // (bm * ROW_TILE)

    def rows(g):
        return pl.ds(pl.multiple_of(g * (bm * ROW_TILE), bm * ROW_TILE), bm * ROW_TILE)

    def in_copy(g, slot):
        return pltpu.make_async_copy(xs_ref.at[rows(g)], xbuf.at[slot], in_sems.at[slot])

    def out_copy(g, slot):
        return pltpu.make_async_copy(obuf.at[slot], o_ref.at[rows(g)], out_sems.at[slot])

    ahead = EXP_IN_SLOTS - 1
    for g_first in range(ahead):
        @pl.when((e == 0) & (nb > g_first))
        def _():
            in_copy(g_first, g_first).start()

    @pl.when(bcnt_ref[e] > 0)
    def _():
        wgb[...] = wg_ref[0].astype(BF16)
        wub[...] = wu_ref[0].astype(BF16)
        wdb[...] = wd_ref[0].astype(BF16)

    def block(j, carry):
        g = g0 + j
        slot = lax.rem(g, 2)
        islot = lax.rem(g, EXP_IN_SLOTS)

        @pl.when(g + ahead < nb)
        def _():
            in_copy(g + ahead, lax.rem(g + ahead, EXP_IN_SLOTS)).start()

        in_copy(g, islot).wait()
        xb = _from_row_tiles(xbuf.at[islot]).astype(BF16)
        gate = _dot(xb, wgb[...])
        up = _dot(xb, wub[...])
        act = (gate * _sigmoid(gate) * up).astype(BF16)
        res = _dot(act, wdb[...])

        @pl.when(g >= 2)
        def _():
            out_copy(g - 2, slot).wait()

        _to_row_tiles(obuf.at[slot], res)
        out_copy(g, slot).start()
        return carry

    lax.fori_loop(0, bcnt_ref[e], block, 0)

    @pl.when(e == pl.num_programs(0) - 1)
    def _():
        @pl.when(nb >= 2)
        def _():
            out_copy(nb - 2, lax.rem(nb, 2)).wait()

        @pl.when(nb >= 1)
        def _():
            out_copy(nb - 1, lax.rem(nb + 1, 2)).wait()

        obuf[0] = jnp.zeros(obuf.shape[1:], obuf.dtype)

        def zstart(g, carry):
            out_copy(g, 0).start()
            return carry

        def zwait(g, carry):
            out_copy(g, 0).wait()
            return carry

        lax.fori_loop(nb, nblk, zstart, 0)
        lax.fori_loop(nb, nblk, zwait, 0)


def _experts(bstart, bcnt, nblk_used, xs, w_gate, w_up, w_down):
    d = D_MODEL
    bm = EXP_BM
    wspec = lambda shape: pl.BlockSpec(shape, lambda e, bs, bc, nb: (e, 0, 0))
    grid_spec = pltpu.PrefetchScalarGridSpec(
        num_scalar_prefetch=3,
        grid=(N_EXPERTS,),
        in_specs=[wspec((1, d, EXPERT_DIM)), wspec((1, d, EXPERT_DIM)), wspec((1, EXPERT_DIM, d)),
                  pl.BlockSpec(memory_space=pl.ANY)],
        out_specs=pl.BlockSpec(memory_space=pl.ANY),
        scratch_shapes=[pltpu.VMEM((EXP_IN_SLOTS, bm * ROW_TILE, LANES), F32),
                        pltpu.VMEM((2, bm * ROW_TILE, LANES), F32),
                        pltpu.VMEM((d, EXPERT_DIM), BF16),
                        pltpu.VMEM((d, EXPERT_DIM), BF16),
                        pltpu.VMEM((EXPERT_DIM, d), BF16),
                        pltpu.SemaphoreType.DMA((EXP_IN_SLOTS,)),
                        pltpu.SemaphoreType.DMA((2,))],
    )
    return pl.pallas_call(
        _experts_kernel,
        out_shape=jax.ShapeDtypeStruct(xs.shape, F32),
        grid_spec=grid_spec,
        compiler_params=_params(("arbitrary",)),
    )(bstart, bcnt, nblk_used, w_gate, w_up, w_down, xs)


def _combine_kernel(dcur_ref, dnxt_ref, wts_ref, base_ref, mod_ref, fg_ref, ob_ref, o_ref, buf, sems):
    tm = CMB_TM
    i = pl.program_id(0)
    slot = lax.rem(i, 2)

    def issue(d_ref, s):
        def body(grp, carry):
            tok0 = pl.multiple_of(grp * SUBLANES, SUBLANES)
            for u in range(SUBLANES):
                for kslot in range(TOP_K):
                    src = pl.multiple_of(d_ref[(tok0 + u) * TOP_K + kslot] * ROW_TILE, ROW_TILE)
                    pltpu.make_async_copy(ob_ref.at[pl.ds(src, ROW_TILE)],
                                          buf.at[s, kslot, pl.ds((tok0 + u) * ROW_TILE, ROW_TILE)],
                                          sems.at[s]).start(priority=kslot % 2)
            return carry
        lax.fori_loop(0, tm // SUBLANES, body, 0)

    @pl.when(i == 0)
    def _():
        issue(dcur_ref, 0)

    for s in range(2):
        @pl.when((i + 1 < pl.num_programs(0)) & (slot == 1 - s))
        def _():
            issue(dnxt_ref, s)

    for kslot in range(TOP_K):
        pltpu.make_async_copy(ob_ref.at[pl.ds(0, tm * ROW_TILE)], buf.at[slot, kslot], sems.at[slot]).wait()

    wts = wts_ref[...]
    acc = wts[:, 0:1] * _from_row_tiles(buf.at[slot, 0])
    for kslot in range(1, TOP_K):
        acc = acc + wts[:, kslot:kslot + 1] * _from_row_tiles(buf.at[slot, kslot])
    x2 = base_ref[...] + mod_ref[0][5:6] * acc
    o_ref[...] = _rmsnorm_rows(x2, fg_ref[...])


def _combine(dest_flat, wts, base, mod, seq, final_g, ob):
    n, d = base.shape
    tm = CMB_TM
    per_b = seq // tm
    nt = n // tm
    return pl.pallas_call(
        _combine_kernel,
        out_shape=jax.ShapeDtypeStruct((n, d), F32),
        grid=(nt,),
        in_specs=[pl.BlockSpec((tm * TOP_K,), lambda i: (i,), memory_space=pltpu.SMEM),
                  pl.BlockSpec((tm * TOP_K,), lambda i: (jnp.minimum(i + 1, nt - 1),),
                               memory_space=pltpu.SMEM),
                  pl.BlockSpec((tm, LANES), lambda i: (i, 0)),
                  pl.BlockSpec((tm, d), lambda i: (i, 0)),
                  pl.BlockSpec((1, 6, d), lambda i: (i // per_b, 0, 0)),
                  pl.BlockSpec((1, d), lambda i: (0, 0)),
                  pl.BlockSpec(memory_space=pl.ANY)],
        out_specs=pl.BlockSpec((tm, d), lambda i: (i, 0)),
        scratch_shapes=[pltpu.VMEM((2, TOP_K, tm * ROW_TILE, LANES), F32), pltpu.SemaphoreType.DMA((2,))],
        compiler_params=_params(("arbitrary",)),
    )(dest_flat, dest_flat, wts, base, mod, final_g.reshape(1, d), ob)


def _layer(x2, bsz, seq, c, w_ada, b_ada, norm1_g, w_in, rw_mu, rw_w0, rw_w2, rw_a0, rw_a2, rw_g2,
           rw_k_k, rw_k_a, rw_r_k, rw_lnx_g, rw_lnx_b, gm_ln_g, gm_ln_b, gm_w_s, gm_b_s,
           w_br_rwkv, w_br_gmlp, w_out, norm2_g, router_w, router_b, moe_w_gate, moe_w_up,
           moe_w_down, sh_w_gate, sh_w_up, sh_w_down, final_g):
    n, d = x2.shape
    mod = _ada(c, w_ada, b_ada).reshape(bsz, 6, d)
    p_rw, gm, gt = _inproj(x2, mod, norm1_g, w_in.astype(BF16), seq)
    o_rw = _rwkv(p_rw, bsz, seq, rw_mu, rw_w0, rw_w2, rw_a0, rw_a2, rw_g2, rw_k_k, rw_k_a,
                 rw_r_k, rw_lnx_g, rw_lnx_b)
    o_gm = _gmlp(gm, gm_ln_g, gm_ln_b, gm_w_s, gm_b_s)
    base, h2f, idx, pos, wts, cnt = _merge(
        x2, o_rw, o_gm, gt, mod, seq, w_br_rwkv.astype(BF16), w_br_gmlp.astype(BF16),
        w_out.astype(BF16), norm2_g, router_w, router_b, sh_w_gate.astype(BF16),
        sh_w_up.astype(BF16), sh_w_down.astype(BF16))

    bm = EXP_BM
    counts = cnt[0].astype(I32)
    padded = (counts + bm - 1) // bm * bm
    pends = jnp.cumsum(padded)
    pstarts = pends - padded
    p_rows = n * TOP_K + N_EXPERTS * bm
    nblk = p_rows // bm
    dest = _destmap(idx, pos, pstarts.astype(F32))[:, :TOP_K].reshape(-1)
    nblk_used = (pends[-1:] // bm).astype(I32)
    tail = nblk_used[0] + jnp.arange(N_EXPERTS, dtype=I32)
    zfill = jnp.concatenate([jnp.where(padded > 0, pends - bm, -1),
                             jnp.where(tail < nblk, tail * bm, -1)]).astype(I32)

    xs = _dispatch(zfill, dest, h2f, p_rows)
    ob = _experts((pstarts // bm).astype(I32), (padded // bm).astype(I32), nblk_used, xs,
                  moe_w_gate, moe_w_up, moe_w_down)
    return _combine(dest, wts, base, mod, seq, final_g, ob)


def kernel(x, c, w_ada, b_ada, norm1_g, w_in, rw_mu, rw_w0, rw_w2, rw_a0, rw_a2, rw_g2, rw_k_k, rw_k_a, rw_r_k, rw_lnx_g, rw_lnx_b, gm_ln_g, gm_ln_b, gm_w_s, gm_b_s, w_br_rwkv, w_br_gmlp, w_out, norm2_g, router_w, router_b, moe_w_gate, moe_w_up, moe_w_down, sh_w_gate, sh_w_up, sh_w_down, final_g):
    bsz, seq, d = x.shape
    assert d == D_MODEL and w_ada.shape[0] == 1, "single-layer block of width D_MODEL"
    assert seq % IN_TM == 0 and seq % RW_BLOCK == 0 and seq % GM_TM == 0 and seq % MG_TM == 0
    assert bsz % RW_NB == 0
    out = _layer(x.reshape(bsz * seq, d), bsz, seq, c, w_ada[0], b_ada[0], norm1_g[0], w_in[0],
                 rw_mu[0], rw_w0[0], rw_w2[0], rw_a0[0], rw_a2[0], rw_g2[0], rw_k_k[0], rw_k_a[0],
                 rw_r_k[0].reshape(-1), rw_lnx_g[0], rw_lnx_b[0], gm_ln_g[0].reshape(-1),
                 gm_ln_b[0].reshape(-1), gm_w_s[0], gm_b_s[0], w_br_rwkv[0], w_br_gmlp[0],
                 w_out[0], norm2_g[0], router_w[0], router_b[0], moe_w_gate[0], moe_w_up[0],
                 moe_w_down[0], sh_w_gate[0], sh_w_up[0], sh_w_down[0], final_g)
    return out.reshape(bsz, seq, d)
```

```python
import functools

import jax
import jax.numpy as jnp
from jax import lax
from jax.experimental import pallas as pl
from jax.experimental.pallas import tpu as pltpu

F32 = jnp.float32
BF16 = jnp.bfloat16
I32 = jnp.int32

D_MODEL = 1024
RW_WIDTH = 512
RW_HEADS = 8
RW_HEAD_DIM = 64
RW_DECAY_LORA = 64
RW_AAA_LORA = 64
RW_GATE_LORA = 128
RW_COLS = 3 * RW_WIDTH + RW_DECAY_LORA + RW_AAA_LORA + RW_GATE_LORA
RW_LNX_EPS = 64e-5
GM_WIDTH = 512
GM_GROUP_DIM = 64
GM_GROUPS = 8
GM_CHUNK = 128
N_EXPERTS = 256
TOP_K = 8
EXPERT_DIM = 256
SHARED_DIM = 256
ROUTED_SCALE = 2.5
NORM_EPS = 1e-6
LN_EPS = 1e-5

LANES = 128
SUBLANES = 8
ROW_TILE = D_MODEL // LANES
VMEM_LIMIT = 56 * 1024 * 1024

WKV_CHUNK = 64
WKV_QUAD = 4
QW = WKV_QUAD * RW_HEAD_DIM
RW_BLOCK = 256
RW_NB = 2
IN_TM = 512
GM_TM = 512
MG_TM = 512
EXP_BM = 256
EXP_IN_SLOTS = 4
DSP_TM = 1024
CMB_TM = 128


def _dot(a, b):
    return jnp.dot(a, b, preferred_element_type=F32)


def _dot_nt(a, b):
    return lax.dot_general(a, b, (((1,), (1,)), ((), ())), preferred_element_type=F32)


def _dot_tn(a, b):
    return lax.dot_general(a, b, (((0,), (0,)), ((), ())), preferred_element_type=F32)


def _split2(a):
    hi = a.astype(BF16)
    lo = (a - hi.astype(F32)).astype(BF16)
    return hi, lo


def _split3(a):
    h1 = a.astype(BF16)
    r1 = a - h1.astype(F32)
    h2 = r1.astype(BF16)
    h3 = (r1 - h2.astype(F32)).astype(BF16)
    return h1, h2, h3


def _dot_hp(a, b):
    ah, al = _split2(a)
    bh, bl = _split2(b)
    return _dot(ah, bh) + _dot(al, bh) + _dot(ah, bl)


def _dot_exact_rhs(a, b_bf16):
    h1, h2, h3 = _split3(a)
    return _dot(h1, b_bf16) + _dot(h2, b_bf16) + _dot(h3, b_bf16)


def _sigmoid(x):
    return 1.0 / (1.0 + jnp.exp(-x))


def _rmsnorm_rows(x, g):
    return x * lax.rsqrt(jnp.mean(x * x, axis=-1, keepdims=True) + NORM_EPS) * g


def _to_row_tiles(ref, x):
    m = x.shape[0]
    for c in range(ROW_TILE):
        ref[pl.ds(c, m, stride=ROW_TILE), :] = x[:, c * LANES:(c + 1) * LANES]


def _from_row_tiles(ref):
    m = ref.shape[0] // ROW_TILE
    return jnp.concatenate([ref[pl.ds(c, m, stride=ROW_TILE), :] for c in range(ROW_TILE)], axis=1)


def _params(sem):
    return pltpu.CompilerParams(dimension_semantics=sem, vmem_limit_bytes=VMEM_LIMIT)


def _ada_kernel(c_ref, w_ref, b_ref, o_ref):
    c = c_ref[...]
    s = c * _sigmoid(c)
    o_ref[...] = _dot_hp(s, w_ref[...]) + b_ref[...]


def _ada(c, w, b):
    bsz, d = c.shape
    n = w.shape[1]
    tn = 1024
    return pl.pallas_call(
        _ada_kernel,
        out_shape=jax.ShapeDtypeStruct((bsz, n), F32),
        grid=(n // tn,),
        in_specs=[pl.BlockSpec((bsz, d), lambda j: (0, 0)),
                  pl.BlockSpec((d, tn), lambda j: (0, j)),
                  pl.BlockSpec((1, tn), lambda j: (0, j))],
        out_specs=pl.BlockSpec((bsz, tn), lambda j: (0, j)),
        compiler_params=_params(("arbitrary",)),
    )(c, w, b.reshape(1, n))


def _inproj_kernel(x_ref, mod_ref, g_ref, w_ref, prw_ref, gm_ref, gt_ref):
    mod = mod_ref[0]
    h = _rmsnorm_rows(x_ref[...], g_ref[...]) * (1.0 + mod[1:2]) + mod[0:1]
    hb = h.astype(BF16)
    prw_ref[...] = _dot(hb, w_ref[:, :RW_COLS])
    pgm = _dot(hb, w_ref[:, RW_COLS:RW_COLS + 2 * GM_WIDTH])
    gm_ref[...] = jax.nn.gelu(pgm, approximate=True).astype(BF16)
    pgt = _dot(hb, w_ref[:, RW_COLS + 2 * GM_WIDTH:])
    gt_ref[...] = _sigmoid(pgt).astype(BF16)


def _inproj(x2, mod, g1, w_in_bf, seq):
    n, d = x2.shape
    tm = IN_TM
    per_b = seq // tm
    cols = w_in_bf.shape[1]
    return pl.pallas_call(
        _inproj_kernel,
        out_shape=(jax.ShapeDtypeStruct((n, RW_COLS), F32),
                   jax.ShapeDtypeStruct((n, 2 * GM_WIDTH), BF16),
                   jax.ShapeDtypeStruct((n, 2 * d), BF16)),
        grid=(n // tm,),
        in_specs=[pl.BlockSpec((tm, d), lambda i: (i, 0)),
                  pl.BlockSpec((1, 6, d), lambda i: (i // per_b, 0, 0)),
                  pl.BlockSpec((1, d), lambda i: (0, 0)),
                  pl.BlockSpec((d, cols), lambda i: (0, 0), pipeline_mode=pl.Buffered(1))],
        out_specs=(pl.BlockSpec((tm, RW_COLS), lambda i: (i, 0)),
                   pl.BlockSpec((tm, 2 * GM_WIDTH), lambda i: (i, 0)),
                   pl.BlockSpec((tm, 2 * d), lambda i: (i, 0))),
        compiler_params=_params(("arbitrary",)),
    )(x2, mod, g1.reshape(1, d), w_in_bf)


def _rwkv_kernel(p_ref, mu_ref, w0_ref, w2_ref, a0_ref, a2_ref, g2_ref, kk_ref, ka_ref,
                 rk_ref, lg_ref, lb_ref, e_ref, o_ref, prev_ref, s_ref):
    nb = RW_NB
    tb = RW_BLOCK
    t = WKV_CHUNK
    c = RW_WIDTH
    nq = RW_HEADS // WKV_QUAD
    nch = tb // t

    @pl.when(pl.program_id(1) == 0)
    def _():
        prev_ref[...] = jnp.zeros_like(prev_ref)
        s_ref[...] = jnp.zeros_like(s_ref)

    p = p_ref[...].reshape(nb * tb, RW_COLS)
    row = lax.broadcasted_iota(I32, p.shape, 0)
    prev = pltpu.roll(p, 1, 0)
    for bi in range(nb):
        prev = jnp.where(row == bi * tb, prev_ref[bi:bi + 1, :], prev)
        prev_ref[bi:bi + 1, :] = p[(bi + 1) * tb - 1:(bi + 1) * tb, :]
    xs = p + (prev - p) * mu_ref[...]
    r = xs[:, 0:c]
    k = xs[:, c:2 * c]
    v = xs[:, 2 * c:3 * c]
    xwa = xs[:, 3 * c:3 * c + LANES]
    xg = xs[:, 3 * c + LANES:3 * c + 2 * LANES]

    wl = w0_ref[...] + _dot_hp(jnp.tanh(xwa), w2_ref[...])
    nz = -wl
    w_log = -(jnp.maximum(nz, 0.0) + jnp.log(1.0 + jnp.exp(-jnp.abs(nz)))) - 0.5
    lw = -jnp.exp(w_log)
    a_sig = _sigmoid(a0_ref[...] + _dot_hp(xwa, a2_ref[...]))
    g = _dot(_sigmoid(xg).astype(BF16), g2_ref[...].astype(BF16))

    e = e_ref[...]

    def headsum(z):
        hi, lo = _split2(z)
        return _dot(hi, e) + _dot(lo, e)

    kk = k * kk_ref[...]
    kk = kk * lax.rsqrt(jnp.maximum(headsum(kk * kk), 1e-24))
    k2 = k * (1.0 + (a_sig - 1.0) * ka_ref[...])
    a_v = -kk
    b_v = kk * a_sig

    ri = lax.broadcasted_iota(I32, (tb, tb), 0)
    ci = lax.broadcasted_iota(I32, (tb, tb), 1)
    tri = jnp.where((ri // t == ci // t) & (ri >= ci), 1.0, 0.0).astype(BF16)
    cum = jnp.concatenate([_dot_exact_lhs(tri, lw[bi * tb:(bi + 1) * tb, :]) for bi in range(nb)], axis=0)
    ends = [cum[(j + 1) * t - 1:(j + 1) * t, :] for j in range(nb * nch)]
    cum_end = jnp.concatenate([jnp.broadcast_to(z, (t, c)) for z in ends], axis=0)

    w_abs = jnp.exp(cum)
    at = (a_v * jnp.exp(cum - lw)).astype(BF16)
    rt = (r * w_abs).astype(BF16)
    w_inv = jnp.exp(-cum)
    kh = (k2 * w_inv).astype(BF16)
    bh = (b_v * w_inv).astype(BF16)
    vb = v.astype(BF16)
    w_rem = jnp.exp(cum_end - cum)
    kw = (k2 * w_rem).astype(BF16)
    bw = (b_v * w_rem).astype(BF16)

    lane = lax.broadcasted_iota(I32, (1, QW), 1)
    hmask = [jnp.where(lane // RW_HEAD_DIM == h, 1.0, 0.0).astype(BF16) for h in range(WKV_QUAD)]

    def stack_heads(z):
        return jnp.concatenate([z * m for m in hmask], axis=0)

    tt = lax.broadcasted_iota(I32, (t, QW), 0)
    ss = lax.broadcasted_iota(I32, (t, QW), 1) % t
    m_strict = tt > ss
    m_incl = tt >= ss
    eye = jnp.where(tt == ss, 1.0, 0.0)
    r4 = lax.broadcasted_iota(I32, (QW, QW), 0)
    c4 = lax.broadcasted_iota(I32, (QW, QW), 1)
    bdm = jnp.where((r4 // t) == (c4 // t), 1.0, 0.0)

    keys = [(bi, ch, q) for bi in range(nb) for ch in range(nch) for q in range(nq)]
    sl = {(bi, ch, q): (slice(bi * tb + ch * t, bi * tb + (ch + 1) * t), slice(q * QW, (q + 1) * QW))
          for bi, ch, q in keys}
    ar = {key: jnp.concatenate([at[sl[key]], rt[sl[key]]], axis=0) for key in keys}
    aa = {key: _dot_nt(ar[key], jnp.concatenate([stack_heads(kh[sl[key]]), stack_heads(bh[sl[key]])],
                                                 axis=0))
          for key in keys}
    a_ak = {key: jnp.where(m_strict, aa[key][0:t, 0:QW], 0.0).astype(BF16) for key in keys}
    a_ab = {key: jnp.where(m_strict, aa[key][0:t, QW:2 * QW], 0.0) for key in keys}
    a_rk = {key: jnp.where(m_incl, aa[key][t:2 * t, 0:QW], 0.0).astype(BF16) for key in keys}
    a_rb = {key: jnp.where(m_incl, aa[key][t:2 * t, QW:2 * QW], 0.0).astype(BF16) for key in keys}
    tinv = {key: eye + a_ab[key] for key in keys}
    pw = {}
    for key in keys:
        lb = a_ab[key].astype(BF16)
        pw[key] = _dot(lb, stack_heads(lb)).astype(BF16)
    for _ in range(4):
        both = {key: _dot(jnp.concatenate([pw[key], tinv[key].astype(BF16)], axis=0), stack_heads(pw[key]))
                for key in keys}
        tinv = {key: tinv[key] + both[key][t:2 * t] for key in keys}
        pw = {key: both[key][0:t].astype(BF16) for key in keys}
    tinv = {key: (tinv[key] + _dot(tinv[key].astype(BF16), stack_heads(pw[key]))).astype(BF16)
            for key in keys}
    avv = {key: _dot(jnp.concatenate([a_ak[key], a_rk[key]], axis=0), stack_heads(vb[sl[key]]))
           for key in keys}

    y_blk = {}
    for ch in range(nch):
        ks = [(bi, ch, q) for bi in range(nb) for q in range(nq)]
        s_q = {key: s_ref[key[0] * nq + key[2]] for key in ks}
        sst = {key: _dot_nt(ar[key], s_q[key].astype(BF16)) for key in ks}
        ub = {key: _dot(tinv[key], stack_heads((sst[key][0:t] + avv[key][0:t]).astype(BF16))).astype(BF16)
              for key in ks}
        upd = {key: _dot_tn(jnp.concatenate([vb[sl[key]], ub[key]], axis=0),
                            jnp.concatenate([kw[sl[key]], bw[sl[key]]], axis=0)) for key in ks}
        for key in ks:
            rs, cs = sl[key]
            w_tot = jnp.exp(ends[key[0] * nch + ch][:, cs])
            s_ref[key[0] * nq + key[2]] = s_q[key] * w_tot + upd[key] * bdm
        for key in ks:
            y_blk[key] = sst[key][t:2 * t] + avv[key][t:2 * t] + _dot(a_rb[key], stack_heads(ub[key]))
    y = jnp.concatenate([jnp.concatenate([y_blk[bi, ch, q] for q in range(nq)], axis=1)
                         for bi in range(nb) for ch in range(nch)], axis=0)

    inv_n = 1.0 / RW_HEAD_DIM
    m = headsum(y) * inv_n
    dlt = y - m
    var = headsum(dlt * dlt) * inv_n
    yn = dlt * lax.rsqrt(var + RW_LNX_EPS) * lg_ref[...] + lb_ref[...]
    bonus = headsum(r * k2 * rk_ref[...]) * v
    o_ref[...] = ((yn + bonus) * g).astype(BF16).reshape(nb, tb, c)


def _dot_exact_lhs(tri_bf16, a):
    h1, h2, h3 = _split3(a)
    return _dot(tri_bf16, h1) + _dot(tri_bf16, h2) + _dot(tri_bf16, h3)


def _rwkv(p_rw, bsz, seq, mu, w0, w2, a0, a2, g2, k_k, k_a, r_k, lnx_g, lnx_b):
    c = RW_WIDTH
    tb = RW_BLOCK
    nb = RW_NB
    zeros = jnp.zeros((RW_DECAY_LORA, c), F32)
    w2p = jnp.concatenate([w2, zeros], axis=0)
    a2p = jnp.concatenate([zeros, a2], axis=0)
    hid = jnp.arange(c, dtype=I32) // RW_HEAD_DIM
    e = (hid[:, None] == hid[None, :]).astype(BF16)
    row = lambda z: z.reshape(1, -1)
    const = lambda shape: pl.BlockSpec(shape, lambda b, j: (0,) * len(shape))
    out = pl.pallas_call(
        _rwkv_kernel,
        out_shape=jax.ShapeDtypeStruct((bsz, seq, c), BF16),
        grid=(bsz // nb, seq // tb),
        in_specs=[pl.BlockSpec((nb, tb, RW_COLS), lambda b, j: (b, j, 0)),
                  const((1, RW_COLS)), const((1, c)), const((LANES, c)), const((1, c)),
                  const((LANES, c)), const((RW_GATE_LORA, c)), const((1, c)), const((1, c)),
                  const((1, c)), const((1, c)), const((1, c)), const((c, c))],
        out_specs=pl.BlockSpec((nb, tb, c), lambda b, j: (b, j, 0)),
        scratch_shapes=[pltpu.VMEM((nb, RW_COLS), F32),
                        pltpu.VMEM((nb * (RW_HEADS // WKV_QUAD), QW, QW), F32)],
        compiler_params=_params(("arbitrary", "arbitrary")),
    )(p_rw.reshape(bsz, seq, RW_COLS), row(mu), row(w0), w2p, row(a0), a2p, g2, row(k_k), row(k_a),
      row(r_k), row(lnx_g), row(lnx_b), e)
    return out.reshape(bsz * seq, c)


def _gmlp_kernel(z_ref, lg_ref, lb_ref, ws_ref, bst_ref, e_ref, o_ref):
    w = GM_WIDTH
    ch = GM_CHUNK
    z = z_ref[...].astype(F32)
    u = z[:, :w]
    v = z[:, w:]
    e = e_ref[...]
    inv_n = 1.0 / GM_GROUP_DIM

    def groupsum(x):
        hi, lo = _split2(x)
        return _dot(hi, e) + _dot(lo, e)

    m = groupsum(v) * inv_n
    dlt = v - m
    var = groupsum(dlt * dlt) * inv_n
    vn = (dlt * lax.rsqrt(var + LN_EPS) * lg_ref[...] + lb_ref[...]).astype(BF16)

    ri = lax.broadcasted_iota(I32, (ch, ch), 0)
    ci = lax.broadcasted_iota(I32, (ch, ch), 1)
    low = ri >= ci
    lane = lax.broadcasted_iota(I32, (1, LANES), 1)
    m_lo = jnp.where(lane < GM_GROUP_DIM, 1.0, 0.0).astype(BF16)
    m_hi = jnp.where(lane >= GM_GROUP_DIM, 1.0, 0.0).astype(BF16)
    bst = bst_ref[...]
    for pr in range(GM_GROUPS // 2):
        g0, g1 = 2 * pr, 2 * pr + 1
        wcat = jnp.concatenate([jnp.where(low, ws_ref[g0], 0.0), jnp.where(low, ws_ref[g1], 0.0)],
                               axis=1).astype(BF16)
        bias = jnp.where(lane < GM_GROUP_DIM, bst[:, g0:g0 + 1], bst[:, g1:g1 + 1])
        ls = slice(pr * LANES, (pr + 1) * LANES)
        for cc in range(z.shape[0] // ch):
            rs = slice(cc * ch, (cc + 1) * ch)
            vp = vn[rs, ls]
            rhs = jnp.concatenate([vp * m_lo, vp * m_hi], axis=0)
            sv = _dot(wcat, rhs) + bias
            o_ref[rs, ls] = (u[rs, ls] * sv).astype(BF16)


def _gmlp(gm, ln_g, ln_b, w_s, b_s):
    n = gm.shape[0]
    w = GM_WIDTH
    tm = GM_TM
    gid = jnp.arange(w, dtype=I32) // GM_GROUP_DIM
    e = (gid[:, None] == gid[None, :]).astype(BF16)
    return pl.pallas_call(
        _gmlp_kernel,
        out_shape=jax.ShapeDtypeStruct((n, w), BF16),
        grid=(n // tm,),
        in_specs=[pl.BlockSpec((tm, 2 * w), lambda i: (i, 0)),
                  pl.BlockSpec((1, w), lambda i: (0, 0)),
                  pl.BlockSpec((1, w), lambda i: (0, 0)),
                  pl.BlockSpec((GM_GROUPS, GM_CHUNK, GM_CHUNK), lambda i: (0, 0, 0)),
                  pl.BlockSpec((GM_CHUNK, GM_GROUPS), lambda i: (0, 0)),
                  pl.BlockSpec((w, w), lambda i: (0, 0))],
        out_specs=pl.BlockSpec((tm, w), lambda i: (i, 0)),
        compiler_params=_params(("arbitrary",)),
    )(gm, ln_g.reshape(1, w), ln_b.reshape(1, w), w_s, b_s.T, e)


def _merge_kernel(x_ref, orw_ref, ogm_ref, gt_ref, mod_ref, wr_ref, wg_ref, wo_ref, g2_ref,
                  rw_ref, rb_ref, sg_ref, su_ref, sd_ref,
                  base_ref, h2_ref, idx_ref, pos_ref, wts_ref, cnt_ref):
    d = D_MODEL
    tm = MG_TM
    ne = N_EXPERTS

    @pl.when(pl.program_id(0) == 0)
    def _():
        cnt_ref[...] = jnp.zeros_like(cnt_ref)

    mod = mod_ref[0]
    gt = gt_ref[...]
    br = _dot(orw_ref[...], wr_ref[...])
    bg = _dot(ogm_ref[...], wg_ref[...])
    merged = gt[:, :d].astype(F32) * br + gt[:, d:].astype(F32) * bg
    x1 = x_ref[...] + mod[2:3] * _dot(merged.astype(BF16), wo_ref[...])
    h2 = _rmsnorm_rows(x1, g2_ref[...]) * (1.0 + mod[4:5]) + mod[3:4]

    _to_row_tiles(h2_ref, h2)
    logits = _dot_hp(h2, rw_ref[...])

    hb = h2.astype(BF16)
    sgate = _dot(hb, sg_ref[...])
    act = (sgate * _sigmoid(sgate) * _dot(hb, su_ref[...])).astype(BF16)
    base_ref[...] = x1 + mod[5:6] * _dot(act, sd_ref[...])

    scores = _sigmoid(logits)
    cur = scores + rb_ref[...]
    lane_e = lax.broadcasted_iota(I32, (tm, ne), 1).astype(F32)
    ri = lax.broadcasted_iota(I32, (tm, tm), 0)
    ci = lax.broadcasted_iota(I32, (tm, tm), 1)
    tri = jnp.where(ri > ci, 1.0, 0.0).astype(BF16)
    onehots, sels, ixs = [], [], []
    for _ in range(TOP_K):
        mx = jnp.max(cur, axis=-1, keepdims=True)
        ix = jnp.min(jnp.where(cur == mx, lane_e, float(ne)), axis=-1, keepdims=True)
        oh = lane_e == ix
        sels.append(jnp.sum(jnp.where(oh, scores, 0.0), axis=-1, keepdims=True))
        cur = jnp.where(oh, -jnp.inf, cur)
        onehots.append(oh)
        ixs.append(ix)
    chosen = onehots[0]
    for oh in onehots[1:]:
        chosen = chosen | oh
    cmask = jnp.where(chosen, 1.0, 0.0)
    denom = sels[0]
    for s in sels[1:]:
        denom = denom + s
    scale = ROUTED_SCALE / denom
    rank = cnt_ref[0:1, :] + _dot(tri, cmask.astype(BF16))
    cnt_ref[0:1, :] = cnt_ref[0:1, :] + jnp.sum(cmask, axis=0, keepdims=True)
    lane_o = lax.broadcasted_iota(I32, (tm, LANES), 1)
    idx_o = jnp.zeros((tm, LANES), F32)
    pos_o = jnp.zeros((tm, LANES), F32)
    wts_o = jnp.zeros((tm, LANES), F32)
    for kslot in range(TOP_K):
        pos_k = jnp.sum(jnp.where(onehots[kslot], rank, 0.0), axis=-1, keepdims=True)
        here = lane_o == kslot
        idx_o = jnp.where(here, ixs[kslot], idx_o)
        pos_o = jnp.where(here, pos_k, pos_o)
        wts_o = jnp.where(here, sels[kslot] * scale, wts_o)
    idx_ref[...] = idx_o.astype(I32)
    pos_ref[...] = pos_o.astype(I32)
    wts_ref[...] = wts_o


def _merge(x2, o_rw, o_gm, gt, mod, seq, wr, wg, wo, g2n, router_w, router_b, sg, su, sd):
    n, d = x2.shape
    tm = MG_TM
    per_b = seq // tm
    ne = N_EXPERTS
    tile = lambda w: pl.BlockSpec((tm, w), lambda i: (i, 0))
    const2 = lambda a, b: pl.BlockSpec((a, b), lambda i: (0, 0))
    return pl.pallas_call(
        _merge_kernel,
        out_shape=(jax.ShapeDtypeStruct((n, d), F32),
                   jax.ShapeDtypeStruct((n * ROW_TILE, LANES), F32),
                   jax.ShapeDtypeStruct((n, LANES), I32),
                   jax.ShapeDtypeStruct((n, LANES), I32),
                   jax.ShapeDtypeStruct((n, LANES), F32),
                   jax.ShapeDtypeStruct((8, ne), F32)),
        grid=(n // tm,),
        in_specs=[tile(d), tile(RW_WIDTH), tile(GM_WIDTH), tile(2 * d),
                  pl.BlockSpec((1, 6, d), lambda i: (i // per_b, 0, 0)),
                  const2(RW_WIDTH, d), const2(GM_WIDTH, d), const2(d, d), const2(1, d),
                  const2(d, ne), const2(1, ne),
                  const2(d, SHARED_DIM), const2(d, SHARED_DIM), const2(SHARED_DIM, d)],
        out_specs=(tile(d), pl.BlockSpec((tm * ROW_TILE, LANES), lambda i: (i, 0)),
                   tile(LANES), tile(LANES), tile(LANES),
                   pl.BlockSpec((8, ne), lambda i: (0, 0))),
        compiler_params=_params(("arbitrary",)),
    )(x2, o_rw, o_gm, gt, mod, wr, wg, wo, g2n.reshape(1, d), router_w, router_b.reshape(1, ne),
      sg, su, sd)


def _dispatch_kernel(zf_ref, ps_ref, idx_ref, pos_ref, h_ref, xs_ref, dest_ref, zbuf, sem):
    tm = DSP_TM
    bm = EXP_BM

    @pl.when(pl.program_id(0) == 0)
    def _():
        zbuf[...] = jnp.zeros_like(zbuf)

        def zero_copy(e):
            first = pl.multiple_of(zf_ref[e] * ROW_TILE, bm * ROW_TILE)
            return pltpu.make_async_copy(zbuf, xs_ref.at[pl.ds(first, bm * ROW_TILE)], sem)

        def zstart(e, carry):
            @pl.when(zf_ref[e] >= 0)
            def _():
                zero_copy(e).start()
            return carry

        def zwait(e, carry):
            @pl.when(zf_ref[e] >= 0)
            def _():
                zero_copy(e).wait()
            return carry

        lax.fori_loop(0, 2 * N_EXPERTS, zstart, 0)
        lax.fori_loop(0, 2 * N_EXPERTS, zwait, 0)

    def issue(grp, carry):
        tok0 = pl.multiple_of(grp * SUBLANES, SUBLANES)
        for u in range(SUBLANES):
            for kslot in range(TOP_K):
                j = (tok0 + u) * TOP_K + kslot
                row = ps_ref[idx_ref[j]] + pos_ref[j]
                dest_ref[j] = row
                dst = pl.multiple_of(row * ROW_TILE, ROW_TILE)
                pltpu.make_async_copy(h_ref.at[pl.ds((tok0 + u) * ROW_TILE, ROW_TILE)],
                                      xs_ref.at[pl.ds(dst, ROW_TILE)],
                                      sem).start(priority=kslot % 2)
        return carry

    lax.fori_loop(0, tm // SUBLANES, issue, 0)
    for kslot in range(TOP_K):
        pltpu.make_async_copy(h_ref, xs_ref.at[pl.ds(0, tm * ROW_TILE)], sem).wait()


def _dispatch(zfill, pstarts, idx_flat, pos_flat, h2f, p_rows):
    n = h2f.shape[0] // ROW_TILE
    tm = DSP_TM
    smem_tile = pl.BlockSpec((tm * TOP_K,), lambda i, zf, ps: (i,), memory_space=pltpu.SMEM)
    grid_spec = pltpu.PrefetchScalarGridSpec(
        num_scalar_prefetch=2,
        grid=(n // tm,),
        in_specs=[smem_tile, smem_tile,
                  pl.BlockSpec((tm * ROW_TILE, LANES), lambda i, zf, ps: (i, 0))],
        out_specs=(pl.BlockSpec(memory_space=pl.ANY), smem_tile),
        scratch_shapes=[pltpu.VMEM((EXP_BM * ROW_TILE, LANES), F32), pltpu.SemaphoreType.DMA(())],
    )
    return pl.pallas_call(
        _dispatch_kernel,
        out_shape=(jax.ShapeDtypeStruct((p_rows * ROW_TILE, LANES), F32),
                   jax.ShapeDtypeStruct((n * TOP_K,), I32)),
        grid_spec=grid_spec,
        compiler_params=_params(("arbitrary",)),
    )(zfill, pstarts, idx_flat, pos_flat, h2f)


def _experts_kernel(bstart_ref, bcnt_ref, nb_ref, wg_ref, wu_ref, wd_ref, xs_ref, o_ref,
                    xbuf, obuf, act_scr, wgb, wub, wdb, in_sems, out_sems):
    e = pl.program_id(0)
    bm = EXP_BM
    nb = nb_ref[0]
    g0 = bstart_ref[e]
    nblk = o_ref.shape[0] // (bm * ROW_TILE)

    def rows(g):
        return pl.ds(pl.multiple_of(g * (bm * ROW_TILE), bm * ROW_TILE), bm * ROW_TILE)

    def in_copy(g, slot):
        return pltpu.make_async_copy(xs_ref.at[rows(g)], xbuf.at[slot], in_sems.at[slot])

    def out_copy(g, slot):
        return pltpu.make_async_copy(obuf.at[slot], o_ref.at[rows(g)], out_sems.at[slot])

    ahead = EXP_IN_SLOTS - 1
    for g_first in range(ahead):
        @pl.when((e == 0) & (nb > g_first))
        def _():
            in_copy(g_first, g_first).start()

    def fetch_rows(g):
        @pl.when(g + ahead < nb)
        def _():
            in_copy(g + ahead, lax.rem(g + ahead, EXP_IN_SLOTS)).start()

        in_copy(g, lax.rem(g, EXP_IN_SLOTS)).wait()

    def free_out_slot(g):
        @pl.when(g >= 2)
        def _():
            out_copy(g - 2, lax.rem(g, 2)).wait()

    def stage_a(g):
        xb = _from_row_tiles(xbuf.at[lax.rem(g, EXP_IN_SLOTS)]).astype(BF16)
        gate = _dot(xb, wgb[...])
        up = _dot(xb, wub[...])
        return (gate * _sigmoid(gate) * up).astype(BF16)

    def stage_b(g):
        return _dot(act_scr[lax.rem(g, 2)], wdb[...])

    def write_back(g, res):
        slot = lax.rem(g, 2)
        _to_row_tiles(obuf.at[slot], res)
        out_copy(g, slot).start()

    cnt = bcnt_ref[e]

    @pl.when(cnt > 0)
    def _():
        wgb[...] = wg_ref[0].astype(BF16)
        wub[...] = wu_ref[0].astype(BF16)
        wdb[...] = wd_ref[0].astype(BF16)
        fetch_rows(g0)
        act_scr[lax.rem(g0, 2)] = stage_a(g0)

        def body(j, carry):
            g = g0 + j
            fetch_rows(g)
            free_out_slot(g - 1)
            res = stage_b(g - 1)
            act_scr[lax.rem(g, 2)] = stage_a(g)
            write_back(g - 1, res)
            return carry

        lax.fori_loop(1, cnt, body, 0)
        last = g0 + cnt - 1
        free_out_slot(last)
        write_back(last, stage_b(last))

    @pl.when(e == pl.num_programs(0) - 1)
    def _():
        @pl.when(nb >= 2)
        def _():
            out_copy(nb - 2, lax.rem(nb, 2)).wait()

        @pl.when(nb >= 1)
        def _():
            out_copy(nb - 1, lax.rem(nb + 1, 2)).wait()

        obuf[0] = jnp.zeros(obuf.shape[1:], obuf.dtype)

        def zstart(g, carry):
            out_copy(g, 0).start()
            return carry

        def zwait(g, carry):
            out_copy(g, 0).wait()
            return carry

        lax.fori_loop(nb, nblk, zstart, 0)
        lax.fori_loop(nb, nblk, zwait, 0)


def _experts(bstart, bcnt, nblk_used, xs, w_gate, w_up, w_down):
    d = D_MODEL
    bm = EXP_BM
    wspec = lambda shape: pl.BlockSpec(shape, lambda e, bs, bc, nb: (e, 0, 0))
    grid_spec = pltpu.PrefetchScalarGridSpec(
        num_scalar_prefetch=3,
        grid=(N_EXPERTS,),
        in_specs=[wspec((1, d, EXPERT_DIM)), wspec((1, d, EXPERT_DIM)), wspec((1, EXPERT_DIM, d)),
                  pl.BlockSpec(memory_space=pl.ANY)],
        out_specs=pl.BlockSpec(memory_space=pl.ANY),
        scratch_shapes=[pltpu.VMEM((EXP_IN_SLOTS, bm * ROW_TILE, LANES), F32),
                        pltpu.VMEM((2, bm * ROW_TILE, LANES), F32),
                        pltpu.VMEM((2, bm, EXPERT_DIM), BF16),
                        pltpu.VMEM((d, EXPERT_DIM), BF16),
                        pltpu.VMEM((d, EXPERT_DIM), BF16),
                        pltpu.VMEM((EXPERT_DIM, d), BF16),
                        pltpu.SemaphoreType.DMA((EXP_IN_SLOTS,)),
                        pltpu.SemaphoreType.DMA((2,))],
    )
    return pl.pallas_call(
        _experts_kernel,
        out_shape=jax.ShapeDtypeStruct(xs.shape, F32),
        grid_spec=grid_spec,
        compiler_params=_params(("arbitrary",)),
    )(bstart, bcnt, nblk_used, w_gate, w_up, w_down, xs)


def _combine_kernel(dcur_ref, dnxt_ref, wts_ref, base_ref, mod_ref, fg_ref, ob_ref, o_ref, buf, sems):
    tm = CMB_TM
    i = pl.program_id(0)
    slot = lax.rem(i, 2)

    def issue(d_ref, s):
        def body(grp, carry):
            tok0 = pl.multiple_of(grp * SUBLANES, SUBLANES)
            for u in range(SUBLANES):
                for kslot in range(TOP_K):
                    src = pl.multiple_of(d_ref[(tok0 + u) * TOP_K + kslot] * ROW_TILE, ROW_TILE)
                    pltpu.make_async_copy(ob_ref.at[pl.ds(src, ROW_TILE)],
                                          buf.at[s, kslot, pl.ds((tok0 + u) * ROW_TILE, ROW_TILE)],
                                          sems.at[s]).start(priority=kslot % 2)
            return carry
        lax.fori_loop(0, tm // SUBLANES, body, 0)

    @pl.when(i == 0)
    def _():
        issue(dcur_ref, 0)

    for s in range(2):
        @pl.when((i + 1 < pl.num_programs(0)) & (slot == 1 - s))
        def _():
            issue(dnxt_ref, s)

    for kslot in range(TOP_K):
        pltpu.make_async_copy(ob_ref.at[pl.ds(0, tm * ROW_TILE)], buf.at[slot, kslot], sems.at[slot]).wait()

    wts = wts_ref[...]
    acc = wts[:, 0:1] * _from_row_tiles(buf.at[slot, 0])
    for kslot in range(1, TOP_K):
        acc = acc + wts[:, kslot:kslot + 1] * _from_row_tiles(buf.at[slot, kslot])
    x2 = base_ref[...] + mod_ref[0][5:6] * acc
    o_ref[...] = _rmsnorm_rows(x2, fg_ref[...])


def _combine(dest_flat, wts, base, mod, seq, final_g, ob):
    n, d = base.shape
    tm = CMB_TM
    per_b = seq // tm
    nt = n // tm
    return pl.pallas_call(
        _combine_kernel,
        out_shape=jax.ShapeDtypeStruct((n, d), F32),
        grid=(nt,),
        in_specs=[pl.BlockSpec((tm * TOP_K,), lambda i: (i,), memory_space=pltpu.SMEM),
                  pl.BlockSpec((tm * TOP_K,), lambda i: (jnp.minimum(i + 1, nt - 1),),
                               memory_space=pltpu.SMEM),
                  pl.BlockSpec((tm, LANES), lambda i: (i, 0)),
                  pl.BlockSpec((tm, d), lambda i: (i, 0)),
                  pl.BlockSpec((1, 6, d), lambda i: (i // per_b, 0, 0)),
                  pl.BlockSpec((1, d), lambda i: (0, 0)),
                  pl.BlockSpec(memory_space=pl.ANY)],
        out_specs=pl.BlockSpec((tm, d), lambda i: (i, 0)),
        scratch_shapes=[pltpu.VMEM((2, TOP_K, tm * ROW_TILE, LANES), F32), pltpu.SemaphoreType.DMA((2,))],
        compiler_params=_params(("arbitrary",)),
    )(dest_flat, dest_flat, wts, base, mod, final_g.reshape(1, d), ob)


def _layer(x2, bsz, seq, c, w_ada, b_ada, norm1_g, w_in, rw_mu, rw_w0, rw_w2, rw_a0, rw_a2, rw_g2,
           rw_k_k, rw_k_a, rw_r_k, rw_lnx_g, rw_lnx_b, gm_ln_g, gm_ln_b, gm_w_s, gm_b_s,
           w_br_rwkv, w_br_gmlp, w_out, norm2_g, router_w, router_b, moe_w_gate, moe_w_up,
           moe_w_down, sh_w_gate, sh_w_up, sh_w_down, final_g):
    n, d = x2.shape
    mod = _ada(c, w_ada, b_ada).reshape(bsz, 6, d)
    p_rw, gm, gt = _inproj(x2, mod, norm1_g, w_in.astype(BF16), seq)
    o_rw = _rwkv(p_rw, bsz, seq, rw_mu, rw_w0, rw_w2, rw_a0, rw_a2, rw_g2, rw_k_k, rw_k_a,
                 rw_r_k, rw_lnx_g, rw_lnx_b)
    o_gm = _gmlp(gm, gm_ln_g, gm_ln_b, gm_w_s, gm_b_s)
    base, h2f, idx, pos, wts, cnt = _merge(
        x2, o_rw, o_gm, gt, mod, seq, w_br_rwkv.astype(BF16), w_br_gmlp.astype(BF16),
        w_out.astype(BF16), norm2_g, router_w, router_b, sh_w_gate.astype(BF16),
        sh_w_up.astype(BF16), sh_w_down.astype(BF16))

    bm = EXP_BM
    counts = cnt[0].astype(I32)
    padded = (counts + bm - 1) // bm * bm
    pends = jnp.cumsum(padded)
    pstarts = pends - padded
    p_rows = n * TOP_K + N_EXPERTS * bm
    nblk = p_rows // bm
    nblk_used = (pends[-1:] // bm).astype(I32)
    tail = nblk_used[0] + jnp.arange(N_EXPERTS, dtype=I32)
    zfill = jnp.concatenate([jnp.where(padded > 0, pends - bm, -1),
                             jnp.where(tail < nblk, tail * bm, -1)]).astype(I32)

    xs, dest = _dispatch(zfill, pstarts.astype(I32), idx[:, :TOP_K].reshape(-1), pos[:, :TOP_K].reshape(-1),
                         h2f, p_rows)
    ob = _experts((pstarts // bm).astype(I32), (padded // bm).astype(I32), nblk_used, xs,
                  moe_w_gate, moe_w_up, moe_w_down)
    return _combine(dest, wts, base, mod, seq, final_g, ob)


def kernel(x, c, w_ada, b_ada, norm1_g, w_in, rw_mu, rw_w0, rw_w2, rw_a0, rw_a2, rw_g2, rw_k_k, rw_k_a, rw_r_k, rw_lnx_g, rw_lnx_b, gm_ln_g, gm_ln_b, gm_w_s, gm_b_s, w_br_rwkv, w_br_gmlp, w_out, norm2_g, router_w, router_b, moe_w_gate, moe_w_up, moe_w_down, sh_w_gate, sh_w_up, sh_w_down, final_g):
    bsz, seq, d = x.shape
    assert d == D_MODEL and w_ada.shape[0] == 1, "single-layer block of width D_MODEL"
    assert seq % IN_TM == 0 and seq % RW_BLOCK == 0 and seq % GM_TM == 0 and seq % MG_TM == 0
    assert bsz % RW_NB == 0
    out = _layer(x.reshape(bsz * seq, d), bsz, seq, c, w_ada[0], b_ada[0], norm1_g[0], w_in[0],
                 rw_mu[0], rw_w0[0], rw_w2[0], rw_a0[0], rw_a2[0], rw_g2[0], rw_k_k[0], rw_k_a[0],
                 rw_r_k[0].reshape(-1), rw_lnx_g[0], rw_lnx_b[0], gm_ln_g[0].reshape(-1),
                 gm_ln_b[0].reshape(-1), gm_w_s[0], gm_b_s[0], w_br_rwkv[0], w_br_gmlp[0],
                 w_out[0], norm2_g[0], router_w[0], router_b[0], moe_w_gate[0], moe_w_up[0],
                 moe_w_down[0], sh_w_gate[0], sh_w_up[0], sh_w_down[0], final_g)
    return out.reshape(bsz, seq, d)
```

```python
import functools

import jax
import jax.numpy as jnp
from jax import lax
from jax.experimental import pallas as pl
from jax.experimental.pallas import tpu as pltpu

F32 = jnp.float32
BF16 = jnp.bfloat16
I32 = jnp.int32

D_MODEL = 1024
RW_WIDTH = 512
RW_HEADS = 8
RW_HEAD_DIM = 64
RW_DECAY_LORA = 64
RW_AAA_LORA = 64
RW_GATE_LORA = 128
RW_COLS = 3 * RW_WIDTH + RW_DECAY_LORA + RW_AAA_LORA + RW_GATE_LORA
RW_LNX_EPS = 64e-5
GM_WIDTH = 512
GM_GROUP_DIM = 64
GM_GROUPS = 8
GM_CHUNK = 128
N_EXPERTS = 256
TOP_K = 8
EXPERT_DIM = 256
SHARED_DIM = 256
ROUTED_SCALE = 2.5
NORM_EPS = 1e-6
LN_EPS = 1e-5

LANES = 128
SUBLANES = 8
ROW_TILE = D_MODEL // LANES
VMEM_LIMIT = 56 * 1024 * 1024

WKV_CHUNK = 64
WKV_QUAD = 4
QW = WKV_QUAD * RW_HEAD_DIM
RW_BLOCK = 256
RW_NB = 2
IN_TM = 512
MG_TM = 512
EXP_BM = 256
EXP_IN_SLOTS = 4
DSP_TM = 2048
CMB_TM = 256


def _dot(a, b):
    return jnp.dot(a, b, preferred_element_type=F32)


def _dot_nt(a, b):
    return lax.dot_general(a, b, (((1,), (1,)), ((), ())), preferred_element_type=F32)


def _dot_tn(a, b):
    return lax.dot_general(a, b, (((0,), (0,)), ((), ())), preferred_element_type=F32)


def _split2(a):
    hi = a.astype(BF16)
    lo = (a - hi.astype(F32)).astype(BF16)
    return hi, lo


def _split3(a):
    h1 = a.astype(BF16)
    r1 = a - h1.astype(F32)
    h2 = r1.astype(BF16)
    h3 = (r1 - h2.astype(F32)).astype(BF16)
    return h1, h2, h3


def _dot_hp(a, b):
    ah, al = _split2(a)
    bh, bl = _split2(b)
    return _dot(ah, bh) + _dot(al, bh) + _dot(ah, bl)


def _dot_exact_rhs(a, b_bf16):
    h1, h2, h3 = _split3(a)
    return _dot(h1, b_bf16) + _dot(h2, b_bf16) + _dot(h3, b_bf16)


def _sigmoid(x):
    return 1.0 / (1.0 + jnp.exp(-x))


def _rmsnorm_rows(x, g):
    return x * lax.rsqrt(jnp.mean(x * x, axis=-1, keepdims=True) + NORM_EPS) * g


def _to_row_tiles(ref, x):
    m = x.shape[0]
    for c in range(ROW_TILE):
        ref[pl.ds(c, m, stride=ROW_TILE), :] = x[:, c * LANES:(c + 1) * LANES]


def _from_row_tiles(ref):
    m = ref.shape[0] // ROW_TILE
    return jnp.concatenate([ref[pl.ds(c, m, stride=ROW_TILE), :] for c in range(ROW_TILE)], axis=1)


def _params(sem):
    return pltpu.CompilerParams(dimension_semantics=sem, vmem_limit_bytes=VMEM_LIMIT)


def _ada_kernel(c_ref, w_ref, b_ref, o_ref):
    c = c_ref[...]
    s = c * _sigmoid(c)
    o_ref[...] = _dot_hp(s, w_ref[...]) + b_ref[...]


def _ada(c, w, b):
    bsz, d = c.shape
    n = w.shape[1]
    tn = 1024
    return pl.pallas_call(
        _ada_kernel,
        out_shape=jax.ShapeDtypeStruct((bsz, n), F32),
        grid=(n // tn,),
        in_specs=[pl.BlockSpec((bsz, d), lambda j: (0, 0)),
                  pl.BlockSpec((d, tn), lambda j: (0, j)),
                  pl.BlockSpec((1, tn), lambda j: (0, j))],
        out_specs=pl.BlockSpec((bsz, tn), lambda j: (0, j)),
        compiler_params=_params(("arbitrary",)),
    )(c, w, b.reshape(1, n))


def _inproj_kernel(x_ref, mod_ref, g_ref, w_ref, lg_ref, lb_ref, ws_ref, bst_ref, e_ref,
                   prw_ref, ogm_ref, gt_ref):
    mod = mod_ref[0]
    h = _rmsnorm_rows(x_ref[...], g_ref[...]) * (1.0 + mod[1:2]) + mod[0:1]
    hb = h.astype(BF16)
    prw_ref[...] = _dot(hb, w_ref[:, :RW_COLS])
    pgm = _dot(hb, w_ref[:, RW_COLS:RW_COLS + 2 * GM_WIDTH])
    _gmlp_tile(jax.nn.gelu(pgm, approximate=True), lg_ref, lb_ref, ws_ref, bst_ref, e_ref, ogm_ref)
    pgt = _dot(hb, w_ref[:, RW_COLS + 2 * GM_WIDTH:])
    gt_ref[...] = _sigmoid(pgt).astype(BF16)


def _inproj(x2, mod, g1, w_in_bf, seq, ln_g, ln_b, w_s, b_s):
    n, d = x2.shape
    tm = IN_TM
    per_b = seq // tm
    cols = w_in_bf.shape[1]
    w = GM_WIDTH
    gid = jnp.arange(w, dtype=I32) // GM_GROUP_DIM
    e = (gid[:, None] == gid[None, :]).astype(BF16)
    const = lambda shape: pl.BlockSpec(shape, lambda i: (0,) * len(shape))
    return pl.pallas_call(
        _inproj_kernel,
        out_shape=(jax.ShapeDtypeStruct((n, RW_COLS), F32),
                   jax.ShapeDtypeStruct((n, w), BF16),
                   jax.ShapeDtypeStruct((n, 2 * d), BF16)),
        grid=(n // tm,),
        in_specs=[pl.BlockSpec((tm, d), lambda i: (i, 0)),
                  pl.BlockSpec((1, 6, d), lambda i: (i // per_b, 0, 0)),
                  const((1, d)),
                  pl.BlockSpec((d, cols), lambda i: (0, 0), pipeline_mode=pl.Buffered(1)),
                  const((1, w)), const((1, w)), const((GM_GROUPS, GM_CHUNK, GM_CHUNK)),
                  const((GM_CHUNK, GM_GROUPS)), const((w, w))],
        out_specs=(pl.BlockSpec((tm, RW_COLS), lambda i: (i, 0)),
                   pl.BlockSpec((tm, w), lambda i: (i, 0)),
                   pl.BlockSpec((tm, 2 * d), lambda i: (i, 0))),
        compiler_params=_params(("arbitrary",)),
    )(x2, mod, g1.reshape(1, d), w_in_bf, ln_g.reshape(1, w), ln_b.reshape(1, w), w_s, b_s.T, e)


def _rwkv_kernel(p_ref, mu_ref, w0_ref, w2_ref, a0_ref, a2_ref, g2_ref, kk_ref, ka_ref,
                 rk_ref, lg_ref, lb_ref, e_ref, o_ref, prev_ref, s_ref):
    nb = RW_NB
    tb = RW_BLOCK
    t = WKV_CHUNK
    c = RW_WIDTH
    nq = RW_HEADS // WKV_QUAD
    nch = tb // t

    @pl.when(pl.program_id(1) == 0)
    def _():
        prev_ref[...] = jnp.zeros_like(prev_ref)
        s_ref[...] = jnp.zeros_like(s_ref)

    p = p_ref[...].reshape(nb * tb, RW_COLS)
    row = lax.broadcasted_iota(I32, p.shape, 0)
    prev = pltpu.roll(p, 1, 0)
    for bi in range(nb):
        prev = jnp.where(row == bi * tb, prev_ref[bi:bi + 1, :], prev)
        prev_ref[bi:bi + 1, :] = p[(bi + 1) * tb - 1:(bi + 1) * tb, :]
    xs = p + (prev - p) * mu_ref[...]
    r = xs[:, 0:c]
    k = xs[:, c:2 * c]
    v = xs[:, 2 * c:3 * c]
    xwa = xs[:, 3 * c:3 * c + LANES]
    xg = xs[:, 3 * c + LANES:3 * c + 2 * LANES]

    wl = w0_ref[...] + _dot_hp(jnp.tanh(xwa), w2_ref[...])
    nz = -wl
    w_log = -(jnp.maximum(nz, 0.0) + jnp.log(1.0 + jnp.exp(-jnp.abs(nz)))) - 0.5
    lw = -jnp.exp(w_log)
    a_sig = _sigmoid(a0_ref[...] + _dot_hp(xwa, a2_ref[...]))
    g = _dot(_sigmoid(xg).astype(BF16), g2_ref[...].astype(BF16))

    e = e_ref[...]

    def headsum(z):
        hi, lo = _split2(z)
        return _dot(hi, e) + _dot(lo, e)

    kk = k * kk_ref[...]
    kk = kk * lax.rsqrt(jnp.maximum(headsum(kk * kk), 1e-24))
    k2 = k * (1.0 + (a_sig - 1.0) * ka_ref[...])
    a_v = -kk
    b_v = kk * a_sig

    ri = lax.broadcasted_iota(I32, (tb, tb), 0)
    ci = lax.broadcasted_iota(I32, (tb, tb), 1)
    tri = jnp.where((ri // t == ci // t) & (ri >= ci), 1.0, 0.0).astype(BF16)
    cum = jnp.concatenate([_dot_exact_lhs(tri, lw[bi * tb:(bi + 1) * tb, :]) for bi in range(nb)], axis=0)
    ends = [cum[(j + 1) * t - 1:(j + 1) * t, :] for j in range(nb * nch)]
    cum_end = jnp.concatenate([jnp.broadcast_to(z, (t, c)) for z in ends], axis=0)

    w_abs = jnp.exp(cum)
    at = (a_v * jnp.exp(cum - lw)).astype(BF16)
    rt = (r * w_abs).astype(BF16)
    w_inv = jnp.exp(-cum)
    kh = (k2 * w_inv).astype(BF16)
    bh = (b_v * w_inv).astype(BF16)
    vb = v.astype(BF16)
    w_rem = jnp.exp(cum_end - cum)
    kw = (k2 * w_rem).astype(BF16)
    bw = (b_v * w_rem).astype(BF16)

    lane = lax.broadcasted_iota(I32, (1, QW), 1)
    hmask = [jnp.where(lane // RW_HEAD_DIM == h, 1.0, 0.0).astype(BF16) for h in range(WKV_QUAD)]

    def stack_heads(z):
        return jnp.concatenate([z * m for m in hmask], axis=0)

    tt = lax.broadcasted_iota(I32, (t, QW), 0)
    ss = lax.broadcasted_iota(I32, (t, QW), 1) % t
    m_strict = tt > ss
    m_incl = tt >= ss
    eye = jnp.where(tt == ss, 1.0, 0.0)
    r4 = lax.broadcasted_iota(I32, (QW, QW), 0)
    c4 = lax.broadcasted_iota(I32, (QW, QW), 1)
    bdm = jnp.where((r4 // t) == (c4 // t), 1.0, 0.0)

    keys = [(bi, ch, q) for bi in range(nb) for ch in range(nch) for q in range(nq)]
    sl = {(bi, ch, q): (slice(bi * tb + ch * t, bi * tb + (ch + 1) * t), slice(q * QW, (q + 1) * QW))
          for bi, ch, q in keys}
    ar = {key: jnp.concatenate([at[sl[key]], rt[sl[key]]], axis=0) for key in keys}
    aa = {key: _dot_nt(ar[key], jnp.concatenate([stack_heads(kh[sl[key]]), stack_heads(bh[sl[key]])],
                                                 axis=0))
          for key in keys}
    a_ak = {key: jnp.where(m_strict, aa[key][0:t, 0:QW], 0.0).astype(BF16) for key in keys}
    a_ab = {key: jnp.where(m_strict, aa[key][0:t, QW:2 * QW], 0.0) for key in keys}
    a_rk = {key: jnp.where(m_incl, aa[key][t:2 * t, 0:QW], 0.0).astype(BF16) for key in keys}
    a_rb = {key: jnp.where(m_incl, aa[key][t:2 * t, QW:2 * QW], 0.0).astype(BF16) for key in keys}
    tinv = {key: eye + a_ab[key] for key in keys}
    pw = {}
    for key in keys:
        lb = a_ab[key].astype(BF16)
        pw[key] = _dot(lb, stack_heads(lb)).astype(BF16)
    for _ in range(4):
        both = {key: _dot(jnp.concatenate([pw[key], tinv[key].astype(BF16)], axis=0), stack_heads(pw[key]))
                for key in keys}
        tinv = {key: tinv[key] + both[key][t:2 * t] for key in keys}
        pw = {key: both[key][0:t].astype(BF16) for key in keys}
    tinv = {key: (tinv[key] + _dot(tinv[key].astype(BF16), stack_heads(pw[key]))).astype(BF16)
            for key in keys}
    avv = {key: _dot(jnp.concatenate([a_ak[key], a_rk[key]], axis=0), stack_heads(vb[sl[key]]))
           for key in keys}

    y_blk = {}
    for ch in range(nch):
        ks = [(bi, ch, q) for bi in range(nb) for q in range(nq)]
        s_q = {key: s_ref[key[0] * nq + key[2]] for key in ks}
        sst = {key: _dot_nt(ar[key], s_q[key].astype(BF16)) for key in ks}
        ub = {key: _dot(tinv[key], stack_heads((sst[key][0:t] + avv[key][0:t]).astype(BF16))).astype(BF16)
              for key in ks}
        upd = {key: _dot_tn(jnp.concatenate([vb[sl[key]], ub[key]], axis=0),
                            jnp.concatenate([kw[sl[key]], bw[sl[key]]], axis=0)) for key in ks}
        for key in ks:
            rs, cs = sl[key]
            w_tot = jnp.exp(ends[key[0] * nch + ch][:, cs])
            s_ref[key[0] * nq + key[2]] = s_q[key] * w_tot + upd[key] * bdm
        for key in ks:
            y_blk[key] = sst[key][t:2 * t] + avv[key][t:2 * t] + _dot(a_rb[key], stack_heads(ub[key]))
    y = jnp.concatenate([jnp.concatenate([y_blk[bi, ch, q] for q in range(nq)], axis=1)
                         for bi in range(nb) for ch in range(nch)], axis=0)

    inv_n = 1.0 / RW_HEAD_DIM
    m = headsum(y) * inv_n
    dlt = y - m
    var = headsum(dlt * dlt) * inv_n
    yn = dlt * lax.rsqrt(var + RW_LNX_EPS) * lg_ref[...] + lb_ref[...]
    bonus = headsum(r * k2 * rk_ref[...]) * v
    o_ref[...] = ((yn + bonus) * g).astype(BF16).reshape(nb, tb, c)


def _dot_exact_lhs(tri_bf16, a):
    h1, h2, h3 = _split3(a)
    return _dot(tri_bf16, h1) + _dot(tri_bf16, h2) + _dot(tri_bf16, h3)


def _rwkv(p_rw, bsz, seq, mu, w0, w2, a0, a2, g2, k_k, k_a, r_k, lnx_g, lnx_b):
    c = RW_WIDTH
    tb = RW_BLOCK
    nb = RW_NB
    zeros = jnp.zeros((RW_DECAY_LORA, c), F32)
    w2p = jnp.concatenate([w2, zeros], axis=0)
    a2p = jnp.concatenate([zeros, a2], axis=0)
    hid = jnp.arange(c, dtype=I32) // RW_HEAD_DIM
    e = (hid[:, None] == hid[None, :]).astype(BF16)
    row = lambda z: z.reshape(1, -1)
    const = lambda shape: pl.BlockSpec(shape, lambda b, j: (0,) * len(shape))
    out = pl.pallas_call(
        _rwkv_kernel,
        out_shape=jax.ShapeDtypeStruct((bsz, seq, c), BF16),
        grid=(bsz // nb, seq // tb),
        in_specs=[pl.BlockSpec((nb, tb, RW_COLS), lambda b, j: (b, j, 0)),
                  const((1, RW_COLS)), const((1, c)), const((LANES, c)), const((1, c)),
                  const((LANES, c)), const((RW_GATE_LORA, c)), const((1, c)), const((1, c)),
                  const((1, c)), const((1, c)), const((1, c)), const((c, c))],
        out_specs=pl.BlockSpec((nb, tb, c), lambda b, j: (b, j, 0)),
        scratch_shapes=[pltpu.VMEM((nb, RW_COLS), F32),
                        pltpu.VMEM((nb * (RW_HEADS // WKV_QUAD), QW, QW), F32)],
        compiler_params=_params(("arbitrary", "arbitrary")),
    )(p_rw.reshape(bsz, seq, RW_COLS), row(mu), row(w0), w2p, row(a0), a2p, g2, row(k_k), row(k_a),
      row(r_k), row(lnx_g), row(lnx_b), e)
    return out.reshape(bsz * seq, c)


def _gmlp_tile(z, lg_ref, lb_ref, ws_ref, bst_ref, e_ref, o_ref):
    w = GM_WIDTH
    ch = GM_CHUNK
    u = z[:, :w]
    v = z[:, w:]
    e = e_ref[...]
    inv_n = 1.0 / GM_GROUP_DIM

    def groupsum(x):
        hi, lo = _split2(x)
        return _dot(hi, e) + _dot(lo, e)

    m = groupsum(v) * inv_n
    dlt = v - m
    var = groupsum(dlt * dlt) * inv_n
    vn = (dlt * lax.rsqrt(var + LN_EPS) * lg_ref[...] + lb_ref[...]).astype(BF16)

    ri = lax.broadcasted_iota(I32, (ch, ch), 0)
    ci = lax.broadcasted_iota(I32, (ch, ch), 1)
    low = ri >= ci
    lane = lax.broadcasted_iota(I32, (1, LANES), 1)
    m_lo = jnp.where(lane < GM_GROUP_DIM, 1.0, 0.0).astype(BF16)
    m_hi = jnp.where(lane >= GM_GROUP_DIM, 1.0, 0.0).astype(BF16)
    bst = bst_ref[...]
    for pr in range(GM_GROUPS // 2):
        g0, g1 = 2 * pr, 2 * pr + 1
        wcat = jnp.concatenate([jnp.where(low, ws_ref[g0], 0.0), jnp.where(low, ws_ref[g1], 0.0)],
                               axis=1).astype(BF16)
        bias = jnp.where(lane < GM_GROUP_DIM, bst[:, g0:g0 + 1], bst[:, g1:g1 + 1])
        ls = slice(pr * LANES, (pr + 1) * LANES)
        for cc in range(z.shape[0] // ch):
            rs = slice(cc * ch, (cc + 1) * ch)
            vp = vn[rs, ls]
            rhs = jnp.concatenate([vp * m_lo, vp * m_hi], axis=0)
            sv = _dot(wcat, rhs) + bias
            o_ref[rs, ls] = (u[rs, ls] * sv).astype(BF16)


def _merge_kernel(x_ref, orw_ref, ogm_ref, gt_ref, mod_ref, wr_ref, wg_ref, wo_ref, g2_ref,
                  rw_ref, rb_ref, sg_ref, su_ref, sd_ref,
                  base_ref, h2_ref, idx_ref, pos_ref, wts_ref, cnt_ref):
    d = D_MODEL
    tm = MG_TM
    ne = N_EXPERTS

    @pl.when(pl.program_id(0) == 0)
    def _():
        cnt_ref[...] = jnp.zeros_like(cnt_ref)

    mod = mod_ref[0]
    gt = gt_ref[...]
    br = _dot(orw_ref[...], wr_ref[...])
    bg = _dot(ogm_ref[...], wg_ref[...])
    merged = gt[:, :d].astype(F32) * br + gt[:, d:].astype(F32) * bg
    x1 = x_ref[...] + mod[2:3] * _dot(merged.astype(BF16), wo_ref[...])
    h2 = _rmsnorm_rows(x1, g2_ref[...]) * (1.0 + mod[4:5]) + mod[3:4]

    _to_row_tiles(h2_ref, h2)
    logits = _dot_hp(h2, rw_ref[...])

    hb = h2.astype(BF16)
    sgate = _dot(hb, sg_ref[...])
    act = (sgate * _sigmoid(sgate) * _dot(hb, su_ref[...])).astype(BF16)
    base_ref[...] = x1 + mod[5:6] * _dot(act, sd_ref[...])

    scores = _sigmoid(logits)
    cur = scores + rb_ref[...]
    lane_e = lax.broadcasted_iota(I32, (tm, ne), 1).astype(F32)
    ri = lax.broadcasted_iota(I32, (tm, tm), 0)
    ci = lax.broadcasted_iota(I32, (tm, tm), 1)
    tri = jnp.where(ri > ci, 1.0, 0.0).astype(BF16)
    onehots, sels, ixs = [], [], []
    for _ in range(TOP_K):
        mx = jnp.max(cur, axis=-1, keepdims=True)
        ix = jnp.min(jnp.where(cur == mx, lane_e, float(ne)), axis=-1, keepdims=True)
        oh = lane_e == ix
        sels.append(jnp.sum(jnp.where(oh, scores, 0.0), axis=-1, keepdims=True))
        cur = jnp.where(oh, -jnp.inf, cur)
        onehots.append(oh)
        ixs.append(ix)
    chosen = onehots[0]
    for oh in onehots[1:]:
        chosen = chosen | oh
    cmask = jnp.where(chosen, 1.0, 0.0)
    denom = sels[0]
    for s in sels[1:]:
        denom = denom + s
    scale = ROUTED_SCALE / denom
    rank = cnt_ref[0:1, :] + _dot(tri, cmask.astype(BF16))
    cnt_ref[0:1, :] = cnt_ref[0:1, :] + jnp.sum(cmask, axis=0, keepdims=True)
    lane_o = lax.broadcasted_iota(I32, (tm, LANES), 1)
    idx_o = jnp.zeros((tm, LANES), F32)
    pos_o = jnp.zeros((tm, LANES), F32)
    wts_o = jnp.zeros((tm, LANES), F32)
    for kslot in range(TOP_K):
        pos_k = jnp.sum(jnp.where(onehots[kslot], rank, 0.0), axis=-1, keepdims=True)
        here = lane_o == kslot
        idx_o = jnp.where(here, ixs[kslot], idx_o)
        pos_o = jnp.where(here, pos_k, pos_o)
        wts_o = jnp.where(here, sels[kslot] * scale, wts_o)
    idx_ref[...] = idx_o.astype(I32)
    pos_ref[...] = pos_o.astype(I32)
    wts_ref[...] = wts_o


def _merge(x2, o_rw, o_gm, gt, mod, seq, wr, wg, wo, g2n, router_w, router_b, sg, su, sd):
    n, d = x2.shape
    tm = MG_TM
    per_b = seq // tm
    ne = N_EXPERTS
    tile = lambda w: pl.BlockSpec((tm, w), lambda i: (i, 0))
    const2 = lambda a, b: pl.BlockSpec((a, b), lambda i: (0, 0))
    return pl.pallas_call(
        _merge_kernel,
        out_shape=(jax.ShapeDtypeStruct((n, d), F32),
                   jax.ShapeDtypeStruct((n * ROW_TILE, LANES), F32),
                   jax.ShapeDtypeStruct((n, LANES), I32),
                   jax.ShapeDtypeStruct((n, LANES), I32),
                   jax.ShapeDtypeStruct((n, LANES), F32),
                   jax.ShapeDtypeStruct((8, ne), F32)),
        grid=(n // tm,),
        in_specs=[tile(d), tile(RW_WIDTH), tile(GM_WIDTH), tile(2 * d),
                  pl.BlockSpec((1, 6, d), lambda i: (i // per_b, 0, 0)),
                  const2(RW_WIDTH, d), const2(GM_WIDTH, d), const2(d, d), const2(1, d),
                  const2(d, ne), const2(1, ne),
                  const2(d, SHARED_DIM), const2(d, SHARED_DIM), const2(SHARED_DIM, d)],
        out_specs=(tile(d), pl.BlockSpec((tm * ROW_TILE, LANES), lambda i: (i, 0)),
                   tile(LANES), tile(LANES), tile(LANES),
                   pl.BlockSpec((8, ne), lambda i: (0, 0))),
        compiler_params=_params(("arbitrary",)),
    )(x2, o_rw, o_gm, gt, mod, wr, wg, wo, g2n.reshape(1, d), router_w, router_b.reshape(1, ne),
      sg, su, sd)


def _destmap_kernel(idx_ref, pos_ref, ps_ref, o_ref):
    tm = idx_ref.shape[0]
    idx = idx_ref[...]
    ps = ps_ref[...]
    lane_e = lax.broadcasted_iota(I32, (tm, N_EXPERTS), 1)
    lane_o = lax.broadcasted_iota(I32, (tm, LANES), 1)
    start = jnp.zeros((tm, LANES), F32)
    for kslot in range(TOP_K):
        hit = lane_e == idx[:, kslot:kslot + 1]
        s_k = jnp.sum(jnp.where(hit, ps, 0.0), axis=-1, keepdims=True)
        start = jnp.where(lane_o == kslot, s_k, start)
    o_ref[...] = pos_ref[...] + start.astype(I32)


def _destmap(idx, pos, pstarts_f32):
    n = idx.shape[0]
    tm = 1024
    return pl.pallas_call(
        _destmap_kernel,
        out_shape=jax.ShapeDtypeStruct((n, LANES), I32),
        grid=(n // tm,),
        in_specs=[pl.BlockSpec((tm, LANES), lambda i: (i, 0)),
                  pl.BlockSpec((tm, LANES), lambda i: (i, 0)),
                  pl.BlockSpec((1, N_EXPERTS), lambda i: (0, 0))],
        out_specs=pl.BlockSpec((tm, LANES), lambda i: (i, 0)),
        compiler_params=_params(("arbitrary",)),
    )(idx, pos, pstarts_f32.reshape(1, N_EXPERTS))


def _dispatch_kernel(zf_ref, dest_ref, h_ref, xs_ref, zbuf, sem):
    tm = DSP_TM
    bm = EXP_BM

    @pl.when(pl.program_id(0) == 0)
    def _():
        zbuf[...] = jnp.zeros_like(zbuf)

        def zero_copy(e):
            first = pl.multiple_of(zf_ref[e] * ROW_TILE, bm * ROW_TILE)
            return pltpu.make_async_copy(zbuf, xs_ref.at[pl.ds(first, bm * ROW_TILE)], sem)

        def zstart(e, carry):
            @pl.when(zf_ref[e] >= 0)
            def _():
                zero_copy(e).start()
            return carry

        def zwait(e, carry):
            @pl.when(zf_ref[e] >= 0)
            def _():
                zero_copy(e).wait()
            return carry

        lax.fori_loop(0, 2 * N_EXPERTS, zstart, 0)
        lax.fori_loop(0, 2 * N_EXPERTS, zwait, 0)

    def issue(grp, carry):
        tok0 = pl.multiple_of(grp * SUBLANES, SUBLANES)
        for u in range(SUBLANES):
            for kslot in range(TOP_K):
                dst = pl.multiple_of(dest_ref[(tok0 + u) * TOP_K + kslot] * ROW_TILE, ROW_TILE)
                pltpu.make_async_copy(h_ref.at[pl.ds((tok0 + u) * ROW_TILE, ROW_TILE)],
                                      xs_ref.at[pl.ds(dst, ROW_TILE)],
                                      sem).start(priority=kslot % 2)
        return carry

    lax.fori_loop(0, tm // SUBLANES, issue, 0)
    for kslot in range(TOP_K):
        pltpu.make_async_copy(h_ref, xs_ref.at[pl.ds(0, tm * ROW_TILE)], sem).wait()


def _dispatch(zfill, dest_flat, h2f, p_rows):
    n = h2f.shape[0] // ROW_TILE
    tm = DSP_TM
    grid_spec = pltpu.PrefetchScalarGridSpec(
        num_scalar_prefetch=1,
        grid=(n // tm,),
        in_specs=[pl.BlockSpec((tm * TOP_K,), lambda i, zf: (i,), memory_space=pltpu.SMEM),
                  pl.BlockSpec((tm * ROW_TILE, LANES), lambda i, zf: (i, 0))],
        out_specs=pl.BlockSpec(memory_space=pl.ANY),
        scratch_shapes=[pltpu.VMEM((EXP_BM * ROW_TILE, LANES), F32), pltpu.SemaphoreType.DMA(())],
    )
    return pl.pallas_call(
        _dispatch_kernel,
        out_shape=jax.ShapeDtypeStruct((p_rows * ROW_TILE, LANES), F32),
        grid_spec=grid_spec,
        compiler_params=_params(("arbitrary",)),
    )(zfill, dest_flat, h2f)


def _experts_kernel(bstart_ref, bcnt_ref, nb_ref, wg_ref, wu_ref, wd_ref, xs_ref, o_ref,
                    xbuf, obuf, act_scr, wgb, wub, wdb, in_sems, out_sems):
    e = pl.program_id(0)
    bm = EXP_BM
    nb = nb_ref[0]
    g0 = bstart_ref[e]
    nblk = o_ref.shape[0] // (bm * ROW_TILE)

    def rows(g):
        return pl.ds(pl.multiple_of(g * (bm * ROW_TILE), bm * ROW_TILE), bm * ROW_TILE)

    def in_copy(g, slot):
        return pltpu.make_async_copy(xs_ref.at[rows(g)], xbuf.at[slot], in_sems.at[slot])

    def out_copy(g, slot):
        return pltpu.make_async_copy(obuf.at[slot], o_ref.at[rows(g)], out_sems.at[slot])

    ahead = EXP_IN_SLOTS - 1
    for g_first in range(ahead):
        @pl.when((e == 0) & (nb > g_first))
        def _():
            in_copy(g_first, g_first).start()

    def fetch_rows(g):
        @pl.when(g + ahead < nb)
        def _():
            in_copy(g + ahead, lax.rem(g + ahead, EXP_IN_SLOTS)).start()

        in_copy(g, lax.rem(g, EXP_IN_SLOTS)).wait()

    def free_out_slot(g):
        @pl.when(g >= 2)
        def _():
            out_copy(g - 2, lax.rem(g, 2)).wait()

    def stage_a(g):
        xb = _from_row_tiles(xbuf.at[lax.rem(g, EXP_IN_SLOTS)]).astype(BF16)
        gate = _dot(xb, wgb[...])
        up = _dot(xb, wub[...])
        return (gate * _sigmoid(gate) * up).astype(BF16)

    def stage_b(g):
        return _dot(act_scr[lax.rem(g, 2)], wdb[...])

    def write_back(g, res):
        slot = lax.rem(g, 2)
        _to_row_tiles(obuf.at[slot], res)
        out_copy(g, slot).start()

    cnt = bcnt_ref[e]

    @pl.when(cnt > 0)
    def _():
        wgb[...] = wg_ref[0].astype(BF16)
        wub[...] = wu_ref[0].astype(BF16)
        wdb[...] = wd_ref[0].astype(BF16)
        fetch_rows(g0)
        act_scr[lax.rem(g0, 2)] = stage_a(g0)

        def body(j, carry):
            g = g0 + j
            fetch_rows(g)
            free_out_slot(g - 1)
            res = stage_b(g - 1)
            act_scr[lax.rem(g, 2)] = stage_a(g)
            write_back(g - 1, res)
            return carry

        lax.fori_loop(1, cnt, body, 0)
        last = g0 + cnt - 1
        free_out_slot(last)
        write_back(last, stage_b(last))

    @pl.when(e == pl.num_programs(0) - 1)
    def _():
        @pl.when(nb >= 2)
        def _():
            out_copy(nb - 2, lax.rem(nb, 2)).wait()

        @pl.when(nb >= 1)
        def _():
            out_copy(nb - 1, lax.rem(nb + 1, 2)).wait()

        obuf[0] = jnp.zeros(obuf.shape[1:], obuf.dtype)

        def zstart(g, carry):
            out_copy(g, 0).start()
            return carry

        def zwait(g, carry):
            out_copy(g, 0).wait()
            return carry

        lax.fori_loop(nb, nblk, zstart, 0)
        lax.fori_loop(nb, nblk, zwait, 0)


def _experts(bstart, bcnt, nblk_used, xs, w_gate, w_up, w_down):
    d = D_MODEL
    bm = EXP_BM
    wspec = lambda shape: pl.BlockSpec(shape, lambda e, bs, bc, nb: (e, 0, 0))
    grid_spec = pltpu.PrefetchScalarGridSpec(
        num_scalar_prefetch=3,
        grid=(N_EXPERTS,),
        in_specs=[wspec((1, d, EXPERT_DIM)), wspec((1, d, EXPERT_DIM)), wspec((1, EXPERT_DIM, d)),
                  pl.BlockSpec(memory_space=pl.ANY)],
        out_specs=pl.BlockSpec(memory_space=pl.ANY),
        scratch_shapes=[pltpu.VMEM((EXP_IN_SLOTS, bm * ROW_TILE, LANES), F32),
                        pltpu.VMEM((2, bm * ROW_TILE, LANES), F32),
                        pltpu.VMEM((2, bm, EXPERT_DIM), BF16),
                        pltpu.VMEM((d, EXPERT_DIM), BF16),
                        pltpu.VMEM((d, EXPERT_DIM), BF16),
                        pltpu.VMEM((EXPERT_DIM, d), BF16),
                        pltpu.SemaphoreType.DMA((EXP_IN_SLOTS,)),
                        pltpu.SemaphoreType.DMA((2,))],
    )
    return pl.pallas_call(
        _experts_kernel,
        out_shape=jax.ShapeDtypeStruct(xs.shape, F32),
        grid_spec=grid_spec,
        compiler_params=_params(("arbitrary",)),
    )(bstart, bcnt, nblk_used, w_gate, w_up, w_down, xs)


def _combine_kernel(dcur_ref, dnxt_ref, wts_ref, base_ref, mod_ref, fg_ref, ob_ref, o_ref, buf, sems):
    tm = CMB_TM
    i = pl.program_id(0)
    slot = lax.rem(i, 2)

    def issue(d_ref, s):
        def body(grp, carry):
            tok0 = pl.multiple_of(grp * SUBLANES, SUBLANES)
            for u in range(SUBLANES):
                for kslot in range(TOP_K):
                    src = pl.multiple_of(d_ref[(tok0 + u) * TOP_K + kslot] * ROW_TILE, ROW_TILE)
                    pltpu.make_async_copy(ob_ref.at[pl.ds(src, ROW_TILE)],
                                          buf.at[s, kslot, pl.ds((tok0 + u) * ROW_TILE, ROW_TILE)],
                                          sems.at[s]).start(priority=kslot % 2)
            return carry
        lax.fori_loop(0, tm // SUBLANES, body, 0)

    @pl.when(i == 0)
    def _():
        issue(dcur_ref, 0)

    for s in range(2):
        @pl.when((i + 1 < pl.num_programs(0)) & (slot == 1 - s))
        def _():
            issue(dnxt_ref, s)

    for kslot in range(TOP_K):
        pltpu.make_async_copy(ob_ref.at[pl.ds(0, tm * ROW_TILE)], buf.at[slot, kslot], sems.at[slot]).wait()

    wts = wts_ref[...]
    acc = wts[:, 0:1] * _from_row_tiles(buf.at[slot, 0])
    for kslot in range(1, TOP_K):
        acc = acc + wts[:, kslot:kslot + 1] * _from_row_tiles(buf.at[slot, kslot])
    x2 = base_ref[...] + mod_ref[0][5:6] * acc
    o_ref[...] = _rmsnorm_rows(x2, fg_ref[...])


def _combine(dest_flat, wts, base, mod, seq, final_g, ob):
    n, d = base.shape
    tm = CMB_TM
    per_b = seq // tm
    nt = n // tm
    return pl.pallas_call(
        _combine_kernel,
        out_shape=jax.ShapeDtypeStruct((n, d), F32),
        grid=(nt,),
        in_specs=[pl.BlockSpec((tm * TOP_K,), lambda i: (i,), memory_space=pltpu.SMEM),
                  pl.BlockSpec((tm * TOP_K,), lambda i: (jnp.minimum(i + 1, nt - 1),),
                               memory_space=pltpu.SMEM),
                  pl.BlockSpec((tm, LANES), lambda i: (i, 0)),
                  pl.BlockSpec((tm, d), lambda i: (i, 0)),
                  pl.BlockSpec((1, 6, d), lambda i: (i // per_b, 0, 0)),
                  pl.BlockSpec((1, d), lambda i: (0, 0)),
                  pl.BlockSpec(memory_space=pl.ANY)],
        out_specs=pl.BlockSpec((tm, d), lambda i: (i, 0)),
        scratch_shapes=[pltpu.VMEM((2, TOP_K, tm * ROW_TILE, LANES), F32), pltpu.SemaphoreType.DMA((2,))],
        compiler_params=_params(("arbitrary",)),
    )(dest_flat, dest_flat, wts, base, mod, final_g.reshape(1, d), ob)


def _layer(x2, bsz, seq, c, w_ada, b_ada, norm1_g, w_in, rw_mu, rw_w0, rw_w2, rw_a0, rw_a2, rw_g2,
           rw_k_k, rw_k_a, rw_r_k, rw_lnx_g, rw_lnx_b, gm_ln_g, gm_ln_b, gm_w_s, gm_b_s,
           w_br_rwkv, w_br_gmlp, w_out, norm2_g, router_w, router_b, moe_w_gate, moe_w_up,
           moe_w_down, sh_w_gate, sh_w_up, sh_w_down, final_g):
    n, d = x2.shape
    mod = _ada(c, w_ada, b_ada).reshape(bsz, 6, d)
    p_rw, o_gm, gt = _inproj(x2, mod, norm1_g, w_in.astype(BF16), seq, gm_ln_g, gm_ln_b, gm_w_s, gm_b_s)
    o_rw = _rwkv(p_rw, bsz, seq, rw_mu, rw_w0, rw_w2, rw_a0, rw_a2, rw_g2, rw_k_k, rw_k_a,
                 rw_r_k, rw_lnx_g, rw_lnx_b)
    base, h2f, idx, pos, wts, cnt = _merge(
        x2, o_rw, o_gm, gt, mod, seq, w_br_rwkv.astype(BF16), w_br_gmlp.astype(BF16),
        w_out.astype(BF16), norm2_g, router_w, router_b, sh_w_gate.astype(BF16),
        sh_w_up.astype(BF16), sh_w_down.astype(BF16))

    bm = EXP_BM
    counts = cnt[0].astype(I32)
    padded = (counts + bm - 1) // bm * bm
    pends = jnp.cumsum(padded)
    pstarts = pends - padded
    p_rows = n * TOP_K + N_EXPERTS * bm
    nblk = p_rows // bm
    nblk_used = (pends[-1:] // bm).astype(I32)
    tail = nblk_used[0] + jnp.arange(N_EXPERTS, dtype=I32)
    zfill = jnp.concatenate([jnp.where(padded > 0, pends - bm, -1),
                             jnp.where(tail < nblk, tail * bm, -1)]).astype(I32)

    dest = _destmap(idx, pos, pstarts.astype(F32))[:, :TOP_K].reshape(-1)
    xs = _dispatch(zfill, dest, h2f, p_rows)
    ob = _experts((pstarts // bm).astype(I32), (padded // bm).astype(I32), nblk_used, xs,
                  moe_w_gate, moe_w_up, moe_w_down)
    return _combine(dest, wts, base, mod, seq, final_g, ob)


def kernel(x, c, w_ada, b_ada, norm1_g, w_in, rw_mu, rw_w0, rw_w2, rw_a0, rw_a2, rw_g2, rw_k_k, rw_k_a, rw_r_k, rw_lnx_g, rw_lnx_b, gm_ln_g, gm_ln_b, gm_w_s, gm_b_s, w_br_rwkv, w_br_gmlp, w_out, norm2_g, router_w, router_b, moe_w_gate, moe_w_up, moe_w_down, sh_w_gate, sh_w_up, sh_w_down, final_g):
    bsz, seq, d = x.shape
    assert d == D_MODEL and w_ada.shape[0] == 1, "single-layer block of width D_MODEL"
    assert seq % IN_TM == 0 and seq % RW_BLOCK == 0 and seq % MG_TM == 0 and IN_TM % GM_CHUNK == 0
    assert bsz % RW_NB == 0 and (bsz * seq) % DSP_TM == 0 and seq % CMB_TM == 0
    out = _layer(x.reshape(bsz * seq, d), bsz, seq, c, w_ada[0], b_ada[0], norm1_g[0], w_in[0],
                 rw_mu[0], rw_w0[0], rw_w2[0], rw_a0[0], rw_a2[0], rw_g2[0], rw_k_k[0], rw_k_a[0],
                 rw_r_k[0].reshape(-1), rw_lnx_g[0], rw_lnx_b[0], gm_ln_g[0].reshape(-1),
                 gm_ln_b[0].reshape(-1), gm_w_s[0], gm_b_s[0], w_br_rwkv[0], w_br_gmlp[0],
                 w_out[0], norm2_g[0], router_w[0], router_b[0], moe_w_gate[0], moe_w_up[0],
                 moe_w_down[0], sh_w_gate[0], sh_w_up[0], sh_w_down[0], final_g)
    return out.reshape(bsz, seq, d)
```

```python
import functools

import jax
import jax.numpy as jnp
from jax import lax
from jax.experimental import pallas as pl
from jax.experimental.pallas import tpu as pltpu

F32 = jnp.float32
BF16 = jnp.bfloat16
I32 = jnp.int32

D_MODEL = 1024
RW_WIDTH = 512
RW_HEADS = 8
RW_HEAD_DIM = 64
RW_DECAY_LORA = 64
RW_AAA_LORA = 64
RW_GATE_LORA = 128
RW_COLS = 3 * RW_WIDTH + RW_DECAY_LORA + RW_AAA_LORA + RW_GATE_LORA
RW_LNX_EPS = 64e-5
GM_WIDTH = 512
GM_GROUP_DIM = 64
GM_GROUPS = 8
GM_CHUNK = 128
N_EXPERTS = 256
TOP_K = 8
EXPERT_DIM = 256
SHARED_DIM = 256
ROUTED_SCALE = 2.5
NORM_EPS = 1e-6
LN_EPS = 1e-5

LANES = 128
SUBLANES = 8
ROW_TILE = D_MODEL // LANES
VMEM_LIMIT = 56 * 1024 * 1024

WKV_CHUNK = 64
WKV_QUAD = 4
QW = WKV_QUAD * RW_HEAD_DIM
RW_BLOCK = 256
RW_NB = 4
IN_TM = 512
MG_TM = 512
EXP_BM = 256
EXP_IN_SLOTS = 4
DSP_TM = 2048
CMB_TM = 256


def _dot(a, b):
    return jnp.dot(a, b, preferred_element_type=F32)


def _dot_nt(a, b):
    return lax.dot_general(a, b, (((1,), (1,)), ((), ())), preferred_element_type=F32)


def _dot_tn(a, b):
    return lax.dot_general(a, b, (((0,), (0,)), ((), ())), preferred_element_type=F32)


def _split2(a):
    hi = a.astype(BF16)
    lo = (a - hi.astype(F32)).astype(BF16)
    return hi, lo


def _split3(a):
    h1 = a.astype(BF16)
    r1 = a - h1.astype(F32)
    h2 = r1.astype(BF16)
    h3 = (r1 - h2.astype(F32)).astype(BF16)
    return h1, h2, h3


def _dot_hp(a, b):
    ah, al = _split2(a)
    bh, bl = _split2(b)
    return _dot(ah, bh) + _dot(al, bh) + _dot(ah, bl)


def _dot_exact_rhs(a, b_bf16):
    h1, h2, h3 = _split3(a)
    return _dot(h1, b_bf16) + _dot(h2, b_bf16) + _dot(h3, b_bf16)


def _sigmoid(x):
    return 1.0 / (1.0 + jnp.exp(-x))


def _rmsnorm_rows(x, g):
    return x * lax.rsqrt(jnp.mean(x * x, axis=-1, keepdims=True) + NORM_EPS) * g


def _to_row_tiles(ref, x):
    m = x.shape[0]
    for c in range(ROW_TILE):
        ref[pl.ds(c, m, stride=ROW_TILE), :] = x[:, c * LANES:(c + 1) * LANES]


def _from_row_tiles(ref):
    m = ref.shape[0] // ROW_TILE
    return jnp.concatenate([ref[pl.ds(c, m, stride=ROW_TILE), :] for c in range(ROW_TILE)], axis=1)


def _params(sem):
    return pltpu.CompilerParams(dimension_semantics=sem, vmem_limit_bytes=VMEM_LIMIT)


def _ada_kernel(c_ref, w_ref, b_ref, o_ref):
    c = c_ref[...]
    s = c * _sigmoid(c)
    o_ref[...] = _dot_hp(s, w_ref[...]) + b_ref[...]


def _ada(c, w, b):
    bsz, d = c.shape
    n = w.shape[1]
    tn = 1024
    return pl.pallas_call(
        _ada_kernel,
        out_shape=jax.ShapeDtypeStruct((bsz, n), F32),
        grid=(n // tn,),
        in_specs=[pl.BlockSpec((bsz, d), lambda j: (0, 0)),
                  pl.BlockSpec((d, tn), lambda j: (0, j)),
                  pl.BlockSpec((1, tn), lambda j: (0, j))],
        out_specs=pl.BlockSpec((bsz, tn), lambda j: (0, j)),
        compiler_params=_params(("arbitrary",)),
    )(c, w, b.reshape(1, n))


def _inproj_kernel(x_ref, mod_ref, g_ref, w_ref, lg_ref, lb_ref, ws_ref, bst_ref, e_ref,
                   prw_ref, ogm_ref, gt_ref):
    mod = mod_ref[0]
    h = _rmsnorm_rows(x_ref[...], g_ref[...]) * (1.0 + mod[1:2]) + mod[0:1]
    hb = h.astype(BF16)
    prw_ref[...] = _dot(hb, w_ref[:, :RW_COLS])
    pgm = _dot(hb, w_ref[:, RW_COLS:RW_COLS + 2 * GM_WIDTH])
    _gmlp_tile(jax.nn.gelu(pgm, approximate=True), lg_ref, lb_ref, ws_ref, bst_ref, e_ref, ogm_ref)
    pgt = _dot(hb, w_ref[:, RW_COLS + 2 * GM_WIDTH:])
    gt_ref[...] = _sigmoid(pgt).astype(BF16)


def _inproj(x2, mod, g1, w_in_bf, seq, ln_g, ln_b, w_s, b_s):
    n, d = x2.shape
    tm = IN_TM
    per_b = seq // tm
    cols = w_in_bf.shape[1]
    w = GM_WIDTH
    gid = jnp.arange(w, dtype=I32) // GM_GROUP_DIM
    e = (gid[:, None] == gid[None, :]).astype(BF16)
    const = lambda shape: pl.BlockSpec(shape, lambda i: (0,) * len(shape))
    return pl.pallas_call(
        _inproj_kernel,
        out_shape=(jax.ShapeDtypeStruct((n, RW_COLS), F32),
                   jax.ShapeDtypeStruct((n, w), BF16),
                   jax.ShapeDtypeStruct((n, 2 * d), BF16)),
        grid=(n // tm,),
        in_specs=[pl.BlockSpec((tm, d), lambda i: (i, 0)),
                  pl.BlockSpec((1, 6, d), lambda i: (i // per_b, 0, 0)),
                  const((1, d)),
                  pl.BlockSpec((d, cols), lambda i: (0, 0), pipeline_mode=pl.Buffered(1)),
                  const((1, w)), const((1, w)), const((GM_GROUPS, GM_CHUNK, GM_CHUNK)),
                  const((GM_CHUNK, GM_GROUPS)), const((w, w))],
        out_specs=(pl.BlockSpec((tm, RW_COLS), lambda i: (i, 0)),
                   pl.BlockSpec((tm, w), lambda i: (i, 0)),
                   pl.BlockSpec((tm, 2 * d), lambda i: (i, 0))),
        compiler_params=_params(("arbitrary",)),
    )(x2, mod, g1.reshape(1, d), w_in_bf, ln_g.reshape(1, w), ln_b.reshape(1, w), w_s, b_s.T, e)


def _rwkv_kernel(p_ref, mu_ref, w0_ref, w2_ref, a0_ref, a2_ref, g2_ref, kk_ref, ka_ref,
                 rk_ref, lg_ref, lb_ref, e_ref, o_ref, prev_ref, s_ref):
    nb = RW_NB
    tb = RW_BLOCK
    t = WKV_CHUNK
    c = RW_WIDTH
    nq = RW_HEADS // WKV_QUAD
    nch = tb // t

    @pl.when(pl.program_id(1) == 0)
    def _():
        prev_ref[...] = jnp.zeros_like(prev_ref)
        s_ref[...] = jnp.zeros_like(s_ref)

    p = p_ref[...].reshape(nb * tb, RW_COLS)
    row = lax.broadcasted_iota(I32, p.shape, 0)
    prev = pltpu.roll(p, 1, 0)
    for bi in range(nb):
        prev = jnp.where(row == bi * tb, prev_ref[bi:bi + 1, :], prev)
        prev_ref[bi:bi + 1, :] = p[(bi + 1) * tb - 1:(bi + 1) * tb, :]
    xs = p + (prev - p) * mu_ref[...]
    r = xs[:, 0:c]
    k = xs[:, c:2 * c]
    v = xs[:, 2 * c:3 * c]
    xwa = xs[:, 3 * c:3 * c + LANES]
    xg = xs[:, 3 * c + LANES:3 * c + 2 * LANES]

    wl = w0_ref[...] + _dot_hp(jnp.tanh(xwa), w2_ref[...])
    nz = -wl
    w_log = -(jnp.maximum(nz, 0.0) + jnp.log(1.0 + jnp.exp(-jnp.abs(nz)))) - 0.5
    lw = -jnp.exp(w_log)
    a_sig = _sigmoid(a0_ref[...] + _dot_hp(xwa, a2_ref[...]))
    g = _dot(_sigmoid(xg).astype(BF16), g2_ref[...].astype(BF16))

    e = e_ref[...]

    def headsum(z):
        hi, lo = _split2(z)
        return _dot(hi, e) + _dot(lo, e)

    kk = k * kk_ref[...]
    kk = kk * lax.rsqrt(jnp.maximum(headsum(kk * kk), 1e-24))
    k2 = k * (1.0 + (a_sig - 1.0) * ka_ref[...])
    a_v = -kk
    b_v = kk * a_sig

    ri = lax.broadcasted_iota(I32, (tb, tb), 0)
    ci = lax.broadcasted_iota(I32, (tb, tb), 1)
    tri = jnp.where((ri // t == ci // t) & (ri >= ci), 1.0, 0.0).astype(BF16)
    cum = jnp.concatenate([_dot_exact_lhs(tri, lw[bi * tb:(bi + 1) * tb, :]) for bi in range(nb)], axis=0)
    ends = [cum[(j + 1) * t - 1:(j + 1) * t, :] for j in range(nb * nch)]
    w_tots = [jnp.exp(z) for z in ends]
    w_tot_rows = jnp.concatenate([jnp.broadcast_to(z, (t, c)) for z in w_tots], axis=0)

    w_abs = jnp.exp(cum)
    at = (a_v * jnp.exp(cum - lw)).astype(BF16)
    rt = (r * w_abs).astype(BF16)
    w_inv = jnp.exp(-cum)
    kh = (k2 * w_inv).astype(BF16)
    bh = (b_v * w_inv).astype(BF16)
    vb = v.astype(BF16)
    w_rem = w_tot_rows * w_inv
    kw = (k2 * w_rem).astype(BF16)
    bw = (b_v * w_rem).astype(BF16)

    lane = lax.broadcasted_iota(I32, (1, QW), 1)
    hmask = [jnp.where(lane // RW_HEAD_DIM == h, 1.0, 0.0).astype(BF16) for h in range(WKV_QUAD)]

    def stack_heads(z):
        return jnp.concatenate([z * m for m in hmask], axis=0)

    tt = lax.broadcasted_iota(I32, (t, QW), 0)
    ss = lax.broadcasted_iota(I32, (t, QW), 1) % t
    m_strict = tt > ss
    m_incl = tt >= ss
    eye = jnp.where(tt == ss, 1.0, 0.0)
    r4 = lax.broadcasted_iota(I32, (QW, QW), 0)
    c4 = lax.broadcasted_iota(I32, (QW, QW), 1)
    bdm = jnp.where((r4 // t) == (c4 // t), 1.0, 0.0)

    keys = [(bi, ch, q) for bi in range(nb) for ch in range(nch) for q in range(nq)]
    sl = {(bi, ch, q): (slice(bi * tb + ch * t, bi * tb + (ch + 1) * t), slice(q * QW, (q + 1) * QW))
          for bi, ch, q in keys}
    ar = {key: jnp.concatenate([at[sl[key]], rt[sl[key]]], axis=0) for key in keys}
    aa = {key: _dot_nt(ar[key], jnp.concatenate([stack_heads(kh[sl[key]]), stack_heads(bh[sl[key]])],
                                                 axis=0))
          for key in keys}
    a_ak = {key: jnp.where(m_strict, aa[key][0:t, 0:QW], 0.0).astype(BF16) for key in keys}
    a_ab = {key: jnp.where(m_strict, aa[key][0:t, QW:2 * QW], 0.0) for key in keys}
    a_rk = {key: jnp.where(m_incl, aa[key][t:2 * t, 0:QW], 0.0).astype(BF16) for key in keys}
    a_rb = {key: jnp.where(m_incl, aa[key][t:2 * t, QW:2 * QW], 0.0).astype(BF16) for key in keys}
    tinv = {key: eye + a_ab[key] for key in keys}
    pw = {}
    for key in keys:
        lb = a_ab[key].astype(BF16)
        pw[key] = _dot(lb, stack_heads(lb)).astype(BF16)
    for _ in range(4):
        both = {key: _dot(jnp.concatenate([pw[key], tinv[key].astype(BF16)], axis=0), stack_heads(pw[key]))
                for key in keys}
        tinv = {key: tinv[key] + both[key][t:2 * t] for key in keys}
        pw = {key: both[key][0:t].astype(BF16) for key in keys}
    tinv = {key: (tinv[key] + _dot(tinv[key].astype(BF16), stack_heads(pw[key]))).astype(BF16)
            for key in keys}
    avv = {key: _dot(jnp.concatenate([a_ak[key], a_rk[key]], axis=0), stack_heads(vb[sl[key]]))
           for key in keys}

    y_blk = {}
    for ch in range(nch):
        ks = [(bi, ch, q) for bi in range(nb) for q in range(nq)]
        s_q = {key: s_ref[key[0] * nq + key[2]] for key in ks}
        sst = {key: _dot_nt(ar[key], s_q[key].astype(BF16)) for key in ks}
        ub = {key: _dot(tinv[key], stack_heads((sst[key][0:t] + avv[key][0:t]).astype(BF16))).astype(BF16)
              for key in ks}
        upd = {key: _dot_tn(jnp.concatenate([vb[sl[key]], ub[key]], axis=0),
                            jnp.concatenate([kw[sl[key]], bw[sl[key]]], axis=0)) for key in ks}
        for key in ks:
            rs, cs = sl[key]
            w_tot = w_tots[key[0] * nch + ch][:, cs]
            s_ref[key[0] * nq + key[2]] = s_q[key] * w_tot + upd[key] * bdm
        for key in ks:
            y_blk[key] = sst[key][t:2 * t] + avv[key][t:2 * t] + _dot(a_rb[key], stack_heads(ub[key]))
    y = jnp.concatenate([jnp.concatenate([y_blk[bi, ch, q] for q in range(nq)], axis=1)
                         for bi in range(nb) for ch in range(nch)], axis=0)

    inv_n = 1.0 / RW_HEAD_DIM
    m = headsum(y) * inv_n
    dlt = y - m
    var = headsum(dlt * dlt) * inv_n
    yn = dlt * lax.rsqrt(var + RW_LNX_EPS) * lg_ref[...] + lb_ref[...]
    bonus = headsum(r * k2 * rk_ref[...]) * v
    o_ref[...] = ((yn + bonus) * g).astype(BF16).reshape(nb, tb, c)


def _dot_exact_lhs(tri_bf16, a):
    h1, h2, h3 = _split3(a)
    return _dot(tri_bf16, h1) + _dot(tri_bf16, h2) + _dot(tri_bf16, h3)


def _rwkv(p_rw, bsz, seq, mu, w0, w2, a0, a2, g2, k_k, k_a, r_k, lnx_g, lnx_b):
    c = RW_WIDTH
    tb = RW_BLOCK
    nb = RW_NB
    zeros = jnp.zeros((RW_DECAY_LORA, c), F32)
    w2p = jnp.concatenate([w2, zeros], axis=0)
    a2p = jnp.concatenate([zeros, a2], axis=0)
    hid = jnp.arange(c, dtype=I32) // RW_HEAD_DIM
    e = (hid[:, None] == hid[None, :]).astype(BF16)
    row = lambda z: z.reshape(1, -1)
    const = lambda shape: pl.BlockSpec(shape, lambda b, j: (0,) * len(shape))
    out = pl.pallas_call(
        _rwkv_kernel,
        out_shape=jax.ShapeDtypeStruct((bsz, seq, c), BF16),
        grid=(bsz // nb, seq // tb),
        in_specs=[pl.BlockSpec((nb, tb, RW_COLS), lambda b, j: (b, j, 0)),
                  const((1, RW_COLS)), const((1, c)), const((LANES, c)), const((1, c)),
                  const((LANES, c)), const((RW_GATE_LORA, c)), const((1, c)), const((1, c)),
                  const((1, c)), const((1, c)), const((1, c)), const((c, c))],
        out_specs=pl.BlockSpec((nb, tb, c), lambda b, j: (b, j, 0)),
        scratch_shapes=[pltpu.VMEM((nb, RW_COLS), F32),
                        pltpu.VMEM((nb * (RW_HEADS // WKV_QUAD), QW, QW), F32)],
        compiler_params=_params(("arbitrary", "arbitrary")),
    )(p_rw.reshape(bsz, seq, RW_COLS), row(mu), row(w0), w2p, row(a0), a2p, g2, row(k_k), row(k_a),
      row(r_k), row(lnx_g), row(lnx_b), e)
    return out.reshape(bsz * seq, c)


def _gmlp_tile(z, lg_ref, lb_ref, ws_ref, bst_ref, e_ref, o_ref):
    w = GM_WIDTH
    ch = GM_CHUNK
    u = z[:, :w]
    v = z[:, w:]
    e = e_ref[...]
    inv_n = 1.0 / GM_GROUP_DIM

    def groupsum(x):
        hi, lo = _split2(x)
        return _dot(hi, e) + _dot(lo, e)

    m = groupsum(v) * inv_n
    dlt = v - m
    var = groupsum(dlt * dlt) * inv_n
    vn = (dlt * lax.rsqrt(var + LN_EPS) * lg_ref[...] + lb_ref[...]).astype(BF16)

    ri = lax.broadcasted_iota(I32, (ch, ch), 0)
    ci = lax.broadcasted_iota(I32, (ch, ch), 1)
    low = ri >= ci
    lane = lax.broadcasted_iota(I32, (1, LANES), 1)
    m_lo = jnp.where(lane < GM_GROUP_DIM, 1.0, 0.0).astype(BF16)
    m_hi = jnp.where(lane >= GM_GROUP_DIM, 1.0, 0.0).astype(BF16)
    bst = bst_ref[...]
    for pr in range(GM_GROUPS // 2):
        g0, g1 = 2 * pr, 2 * pr + 1
        wcat = jnp.concatenate([jnp.where(low, ws_ref[g0], 0.0), jnp.where(low, ws_ref[g1], 0.0)],
                               axis=1).astype(BF16)
        bias = jnp.where(lane < GM_GROUP_DIM, bst[:, g0:g0 + 1], bst[:, g1:g1 + 1])
        ls = slice(pr * LANES, (pr + 1) * LANES)
        for cc in range(z.shape[0] // ch):
            rs = slice(cc * ch, (cc + 1) * ch)
            vp = vn[rs, ls]
            rhs = jnp.concatenate([vp * m_lo, vp * m_hi], axis=0)
            sv = _dot(wcat, rhs) + bias
            o_ref[rs, ls] = (u[rs, ls] * sv).astype(BF16)


def _merge_kernel(x_ref, orw_ref, ogm_ref, gt_ref, mod_ref, wr_ref, wg_ref, wo_ref, g2_ref,
                  rw_ref, rb_ref, sg_ref, su_ref, sd_ref,
                  base_ref, h2_ref, idx_ref, pos_ref, wts_ref, cnt_ref):
    d = D_MODEL
    tm = MG_TM
    ne = N_EXPERTS

    @pl.when(pl.program_id(0) == 0)
    def _():
        cnt_ref[...] = jnp.zeros_like(cnt_ref)

    mod = mod_ref[0]
    gt = gt_ref[...]
    br = _dot(orw_ref[...], wr_ref[...])
    bg = _dot(ogm_ref[...], wg_ref[...])
    merged = gt[:, :d].astype(F32) * br + gt[:, d:].astype(F32) * bg
    x1 = x_ref[...] + mod[2:3] * _dot(merged.astype(BF16), wo_ref[...])
    h2 = _rmsnorm_rows(x1, g2_ref[...]) * (1.0 + mod[4:5]) + mod[3:4]

    _to_row_tiles(h2_ref, h2)
    logits = _dot_hp(h2, rw_ref[...])

    hb = h2.astype(BF16)
    sgate = _dot(hb, sg_ref[...])
    act = (sgate * _sigmoid(sgate) * _dot(hb, su_ref[...])).astype(BF16)
    base_ref[...] = x1 + mod[5:6] * _dot(act, sd_ref[...])

    scores = _sigmoid(logits)
    cur = scores + rb_ref[...]
    lane_e = lax.broadcasted_iota(I32, (tm, ne), 1).astype(F32)
    ri = lax.broadcasted_iota(I32, (tm, tm), 0)
    ci = lax.broadcasted_iota(I32, (tm, tm), 1)
    tri = jnp.where(ri > ci, 1.0, 0.0).astype(BF16)
    onehots, sels, ixs = [], [], []
    for _ in range(TOP_K):
        mx = jnp.max(cur, axis=-1, keepdims=True)
        ix = jnp.min(jnp.where(cur == mx, lane_e, float(ne)), axis=-1, keepdims=True)
        oh = lane_e == ix
        sels.append(jnp.sum(jnp.where(oh, scores, 0.0), axis=-1, keepdims=True))
        cur = jnp.where(oh, -jnp.inf, cur)
        onehots.append(oh)
        ixs.append(ix)
    chosen = onehots[0]
    for oh in onehots[1:]:
        chosen = chosen | oh
    cmask = jnp.where(chosen, 1.0, 0.0)
    denom = sels[0]
    for s in sels[1:]:
        denom = denom + s
    scale = ROUTED_SCALE / denom
    rank = cnt_ref[0:1, :] + _dot(tri, cmask.astype(BF16))
    cnt_ref[0:1, :] = cnt_ref[0:1, :] + jnp.sum(cmask, axis=0, keepdims=True)
    lane_o = lax.broadcasted_iota(I32, (tm, LANES), 1)
    idx_o = jnp.zeros((tm, LANES), F32)
    pos_o = jnp.zeros((tm, LANES), F32)
    wts_o = jnp.zeros((tm, LANES), F32)
    for kslot in range(TOP_K):
        pos_k = jnp.sum(jnp.where(onehots[kslot], rank, 0.0), axis=-1, keepdims=True)
        here = lane_o == kslot
        idx_o = jnp.where(here, ixs[kslot], idx_o)
        pos_o = jnp.where(here, pos_k, pos_o)
        wts_o = jnp.where(here, sels[kslot] * scale, wts_o)
    idx_ref[...] = idx_o.astype(I32)
    pos_ref[...] = pos_o.astype(I32)
    wts_ref[...] = wts_o


def _merge(x2, o_rw, o_gm, gt, mod, seq, wr, wg, wo, g2n, router_w, router_b, sg, su, sd):
    n, d = x2.shape
    tm = MG_TM
    per_b = seq // tm
    ne = N_EXPERTS
    tile = lambda w: pl.BlockSpec((tm, w), lambda i: (i, 0))
    const2 = lambda a, b: pl.BlockSpec((a, b), lambda i: (0, 0))
    return pl.pallas_call(
        _merge_kernel,
        out_shape=(jax.ShapeDtypeStruct((n, d), F32),
                   jax.ShapeDtypeStruct((n * ROW_TILE, LANES), F32),
                   jax.ShapeDtypeStruct((n, LANES), I32),
                   jax.ShapeDtypeStruct((n, LANES), I32),
                   jax.ShapeDtypeStruct((n, LANES), F32),
                   jax.ShapeDtypeStruct((8, ne), F32)),
        grid=(n // tm,),
        in_specs=[tile(d), tile(RW_WIDTH), tile(GM_WIDTH), tile(2 * d),
                  pl.BlockSpec((1, 6, d), lambda i: (i // per_b, 0, 0)),
                  const2(RW_WIDTH, d), const2(GM_WIDTH, d), const2(d, d), const2(1, d),
                  const2(d, ne), const2(1, ne),
                  const2(d, SHARED_DIM), const2(d, SHARED_DIM), const2(SHARED_DIM, d)],
        out_specs=(tile(d), pl.BlockSpec((tm * ROW_TILE, LANES), lambda i: (i, 0)),
                   tile(LANES), tile(LANES), tile(LANES),
                   pl.BlockSpec((8, ne), lambda i: (0, 0))),
        compiler_params=_params(("arbitrary",)),
    )(x2, o_rw, o_gm, gt, mod, wr, wg, wo, g2n.reshape(1, d), router_w, router_b.reshape(1, ne),
      sg, su, sd)


def _destmap_kernel(idx_ref, pos_ref, ps_ref, o_ref):
    tm = idx_ref.shape[0]
    idx = idx_ref[...]
    ps = ps_ref[...]
    lane_e = lax.broadcasted_iota(I32, (tm, N_EXPERTS), 1)
    lane_o = lax.broadcasted_iota(I32, (tm, LANES), 1)
    start = jnp.zeros((tm, LANES), F32)
    for kslot in range(TOP_K):
        hit = lane_e == idx[:, kslot:kslot + 1]
        s_k = jnp.sum(jnp.where(hit, ps, 0.0), axis=-1, keepdims=True)
        start = jnp.where(lane_o == kslot, s_k, start)
    o_ref[...] = pos_ref[...] + start.astype(I32)


def _destmap(idx, pos, pstarts_f32):
    n = idx.shape[0]
    tm = 1024
    return pl.pallas_call(
        _destmap_kernel,
        out_shape=jax.ShapeDtypeStruct((n, LANES), I32),
        grid=(n // tm,),
        in_specs=[pl.BlockSpec((tm, LANES), lambda i: (i, 0)),
                  pl.BlockSpec((tm, LANES), lambda i: (i, 0)),
                  pl.BlockSpec((1, N_EXPERTS), lambda i: (0, 0))],
        out_specs=pl.BlockSpec((tm, LANES), lambda i: (i, 0)),
        compiler_params=_params(("arbitrary",)),
    )(idx, pos, pstarts_f32.reshape(1, N_EXPERTS))


def _dispatch_kernel(zf_ref, dest_ref, h_ref, xs_ref, zbuf, sem):
    tm = DSP_TM
    bm = EXP_BM

    @pl.when(pl.program_id(0) == 0)
    def _():
        zbuf[...] = jnp.zeros_like(zbuf)

        def zero_copy(e):
            first = pl.multiple_of(zf_ref[e] * ROW_TILE, bm * ROW_TILE)
            return pltpu.make_async_copy(zbuf, xs_ref.at[pl.ds(first, bm * ROW_TILE)], sem)

        def zstart(e, carry):
            @pl.when(zf_ref[e] >= 0)
            def _():
                zero_copy(e).start()
            return carry

        def zwait(e, carry):
            @pl.when(zf_ref[e] >= 0)
            def _():
                zero_copy(e).wait()
            return carry

        lax.fori_loop(0, 2 * N_EXPERTS, zstart, 0)
        lax.fori_loop(0, 2 * N_EXPERTS, zwait, 0)

    def issue(grp, carry):
        tok0 = pl.multiple_of(grp * SUBLANES, SUBLANES)
        for u in range(SUBLANES):
            for kslot in range(TOP_K):
                dst = pl.multiple_of(dest_ref[(tok0 + u) * TOP_K + kslot] * ROW_TILE, ROW_TILE)
                pltpu.make_async_copy(h_ref.at[pl.ds((tok0 + u) * ROW_TILE, ROW_TILE)],
                                      xs_ref.at[pl.ds(dst, ROW_TILE)],
                                      sem).start(priority=kslot % 2)
        return carry

    lax.fori_loop(0, tm // SUBLANES, issue, 0)
    for kslot in range(TOP_K):
        pltpu.make_async_copy(h_ref, xs_ref.at[pl.ds(0, tm * ROW_TILE)], sem).wait()


def _dispatch(zfill, dest_flat, h2f, p_rows):
    n = h2f.shape[0] // ROW_TILE
    tm = DSP_TM
    grid_spec = pltpu.PrefetchScalarGridSpec(
        num_scalar_prefetch=1,
        grid=(n // tm,),
        in_specs=[pl.BlockSpec((tm * TOP_K,), lambda i, zf: (i,), memory_space=pltpu.SMEM),
                  pl.BlockSpec((tm * ROW_TILE, LANES), lambda i, zf: (i, 0))],
        out_specs=pl.BlockSpec(memory_space=pl.ANY),
        scratch_shapes=[pltpu.VMEM((EXP_BM * ROW_TILE, LANES), F32), pltpu.SemaphoreType.DMA(())],
    )
    return pl.pallas_call(
        _dispatch_kernel,
        out_shape=jax.ShapeDtypeStruct((p_rows * ROW_TILE, LANES), F32),
        grid_spec=grid_spec,
        compiler_params=_params(("arbitrary",)),
    )(zfill, dest_flat, h2f)


def _experts_kernel(bstart_ref, bcnt_ref, nb_ref, wg_ref, wu_ref, wd_ref, xs_ref, o_ref,
                    xbuf, obuf, act_scr, wgb, wub, wdb, in_sems, out_sems):
    e = pl.program_id(0)
    bm = EXP_BM
    nb = nb_ref[0]
    g0 = bstart_ref[e]
    nblk = o_ref.shape[0] // (bm * ROW_TILE)

    def rows(g):
        return pl.ds(pl.multiple_of(g * (bm * ROW_TILE), bm * ROW_TILE), bm * ROW_TILE)

    def in_copy(g, slot):
        return pltpu.make_async_copy(xs_ref.at[rows(g)], xbuf.at[slot], in_sems.at[slot])

    def out_copy(g, slot):
        return pltpu.make_async_copy(obuf.at[slot], o_ref.at[rows(g)], out_sems.at[slot])

    ahead = EXP_IN_SLOTS - 1
    for g_first in range(ahead):
        @pl.when((e == 0) & (nb > g_first))
        def _():
            in_copy(g_first, g_first).start()

    def fetch_rows(g):
        @pl.when(g + ahead < nb)
        def _():
            in_copy(g + ahead, lax.rem(g + ahead, EXP_IN_SLOTS)).start()

        in_copy(g, lax.rem(g, EXP_IN_SLOTS)).wait()

    def free_out_slot(g):
        @pl.when(g >= 2)
        def _():
            out_copy(g - 2, lax.rem(g, 2)).wait()

    def stage_a(g):
        xb = _from_row_tiles(xbuf.at[lax.rem(g, EXP_IN_SLOTS)]).astype(BF16)
        gate = _dot(xb, wgb[...])
        up = _dot(xb, wub[...])
        return (gate * _sigmoid(gate) * up).astype(BF16)

    def stage_b(g):
        return _dot(act_scr[lax.rem(g, 2)], wdb[...])

    def write_back(g, res):
        slot = lax.rem(g, 2)
        _to_row_tiles(obuf.at[slot], res)
        out_copy(g, slot).start()

    cnt = bcnt_ref[e]

    @pl.when(cnt > 0)
    def _():
        wgb[...] = wg_ref[0].astype(BF16)
        wub[...] = wu_ref[0].astype(BF16)
        wdb[...] = wd_ref[0].astype(BF16)
        fetch_rows(g0)
        act_scr[lax.rem(g0, 2)] = stage_a(g0)

        def body(j, carry):
            g = g0 + j
            fetch_rows(g)
            free_out_slot(g - 1)
            res = stage_b(g - 1)
            act_scr[lax.rem(g, 2)] = stage_a(g)
            write_back(g - 1, res)
            return carry

        lax.fori_loop(1, cnt, body, 0)
        last = g0 + cnt - 1
        free_out_slot(last)
        write_back(last, stage_b(last))

    @pl.when(e == pl.num_programs(0) - 1)
    def _():
        @pl.when(nb >= 2)
        def _():
            out_copy(nb - 2, lax.rem(nb, 2)).wait()

        @pl.when(nb >= 1)
        def _():
            out_copy(nb - 1, lax.rem(nb + 1, 2)).wait()

        obuf[0] = jnp.zeros(obuf.shape[1:], obuf.dtype)

        def zstart(g, carry):
            out_copy(g, 0).start()
            return carry

        def zwait(g, carry):
            out_copy(g, 0).wait()
            return carry

        lax.fori_loop(nb, nblk, zstart, 0)
        lax.fori_loop(nb, nblk, zwait, 0)


def _experts(bstart, bcnt, nblk_used, xs, w_gate, w_up, w_down):
    d = D_MODEL
    bm = EXP_BM
    wspec = lambda shape: pl.BlockSpec(shape, lambda e, bs, bc, nb: (e, 0, 0))
    grid_spec = pltpu.PrefetchScalarGridSpec(
        num_scalar_prefetch=3,
        grid=(N_EXPERTS,),
        in_specs=[wspec((1, d, EXPERT_DIM)), wspec((1, d, EXPERT_DIM)), wspec((1, EXPERT_DIM, d)),
                  pl.BlockSpec(memory_space=pl.ANY)],
        out_specs=pl.BlockSpec(memory_space=pl.ANY),
        scratch_shapes=[pltpu.VMEM((EXP_IN_SLOTS, bm * ROW_TILE, LANES), F32),
                        pltpu.VMEM((2, bm * ROW_TILE, LANES), F32),
                        pltpu.VMEM((2, bm, EXPERT_DIM), BF16),
                        pltpu.VMEM((d, EXPERT_DIM), BF16),
                        pltpu.VMEM((d, EXPERT_DIM), BF16),
                        pltpu.VMEM((EXPERT_DIM, d), BF16),
                        pltpu.SemaphoreType.DMA((EXP_IN_SLOTS,)),
                        pltpu.SemaphoreType.DMA((2,))],
    )
    return pl.pallas_call(
        _experts_kernel,
        out_shape=jax.ShapeDtypeStruct(xs.shape, F32),
        grid_spec=grid_spec,
        compiler_params=_params(("arbitrary",)),
    )(bstart, bcnt, nblk_used, w_gate, w_up, w_down, xs)


def _combine_kernel(dcur_ref, dnxt_ref, wts_ref, base_ref, mod_ref, fg_ref, ob_ref, o_ref, buf, sems):
    tm = CMB_TM
    i = pl.program_id(0)
    slot = lax.rem(i, 2)

    def issue(d_ref, s):
        def body(grp, carry):
            tok0 = pl.multiple_of(grp * SUBLANES, SUBLANES)
            for u in range(SUBLANES):
                for kslot in range(TOP_K):
                    src = pl.multiple_of(d_ref[(tok0 + u) * TOP_K + kslot] * ROW_TILE, ROW_TILE)
                    pltpu.make_async_copy(ob_ref.at[pl.ds(src, ROW_TILE)],
                                          buf.at[s, kslot, pl.ds((tok0 + u) * ROW_TILE, ROW_TILE)],
                                          sems.at[s]).start(priority=kslot % 2)
            return carry
        lax.fori_loop(0, tm // SUBLANES, body, 0)

    @pl.when(i == 0)
    def _():
        issue(dcur_ref, 0)

    for s in range(2):
        @pl.when((i + 1 < pl.num_programs(0)) & (slot == 1 - s))
        def _():
            issue(dnxt_ref, s)

    for kslot in range(TOP_K):
        pltpu.make_async_copy(ob_ref.at[pl.ds(0, tm * ROW_TILE)], buf.at[slot, kslot], sems.at[slot]).wait()

    wts = wts_ref[...]
    acc = wts[:, 0:1] * _from_row_tiles(buf.at[slot, 0])
    for kslot in range(1, TOP_K):
        acc = acc + wts[:, kslot:kslot + 1] * _from_row_tiles(buf.at[slot, kslot])
    x2 = base_ref[...] + mod_ref[0][5:6] * acc
    o_ref[...] = _rmsnorm_rows(x2, fg_ref[...])


def _combine(dest_flat, wts, base, mod, seq, final_g, ob):
    n, d = base.shape
    tm = CMB_TM
    per_b = seq // tm
    nt = n // tm
    return pl.pallas_call(
        _combine_kernel,
        out_shape=jax.ShapeDtypeStruct((n, d), F32),
        grid=(nt,),
        in_specs=[pl.BlockSpec((tm * TOP_K,), lambda i: (i,), memory_space=pltpu.SMEM),
                  pl.BlockSpec((tm * TOP_K,), lambda i: (jnp.minimum(i + 1, nt - 1),),
                               memory_space=pltpu.SMEM),
                  pl.BlockSpec((tm, LANES), lambda i: (i, 0)),
                  pl.BlockSpec((tm, d), lambda i: (i, 0)),
                  pl.BlockSpec((1, 6, d), lambda i: (i // per_b, 0, 0)),
                  pl.BlockSpec((1, d), lambda i: (0, 0)),
                  pl.BlockSpec(memory_space=pl.ANY)],
        out_specs=pl.BlockSpec((tm, d), lambda i: (i, 0)),
        scratch_shapes=[pltpu.VMEM((2, TOP_K, tm * ROW_TILE, LANES), F32), pltpu.SemaphoreType.DMA((2,))],
        compiler_params=_params(("arbitrary",)),
    )(dest_flat, dest_flat, wts, base, mod, final_g.reshape(1, d), ob)


def _layer(x2, bsz, seq, c, w_ada, b_ada, norm1_g, w_in, rw_mu, rw_w0, rw_w2, rw_a0, rw_a2, rw_g2,
           rw_k_k, rw_k_a, rw_r_k, rw_lnx_g, rw_lnx_b, gm_ln_g, gm_ln_b, gm_w_s, gm_b_s,
           w_br_rwkv, w_br_gmlp, w_out, norm2_g, router_w, router_b, moe_w_gate, moe_w_up,
           moe_w_down, sh_w_gate, sh_w_up, sh_w_down, final_g):
    n, d = x2.shape
    mod = _ada(c, w_ada, b_ada).reshape(bsz, 6, d)
    p_rw, o_gm, gt = _inproj(x2, mod, norm1_g, w_in.astype(BF16), seq, gm_ln_g, gm_ln_b, gm_w_s, gm_b_s)
    o_rw = _rwkv(p_rw, bsz, seq, rw_mu, rw_w0, rw_w2, rw_a0, rw_a2, rw_g2, rw_k_k, rw_k_a,
                 rw_r_k, rw_lnx_g, rw_lnx_b)
    base, h2f, idx, pos, wts, cnt = _merge(
        x2, o_rw, o_gm, gt, mod, seq, w_br_rwkv.astype(BF16), w_br_gmlp.astype(BF16),
        w_out.astype(BF16), norm2_g, router_w, router_b, sh_w_gate.astype(BF16),
        sh_w_up.astype(BF16), sh_w_down.astype(BF16))

    bm = EXP_BM
    counts = cnt[0].astype(I32)
    padded = (counts + bm - 1) // bm * bm
    pends = jnp.cumsum(padded)
    pstarts = pends - padded
    p_rows = n * TOP_K + N_EXPERTS * bm
    nblk = p_rows // bm
    nblk_used = (pends[-1:] // bm).astype(I32)
    tail = nblk_used[0] + jnp.arange(N_EXPERTS, dtype=I32)
    zfill = jnp.concatenate([jnp.where(padded > 0, pends - bm, -1),
                             jnp.where(tail < nblk, tail * bm, -1)]).astype(I32)

    dest = _destmap(idx, pos, pstarts.astype(F32))[:, :TOP_K].reshape(-1)
    xs = _dispatch(zfill, dest, h2f, p_rows)
    ob = _experts((pstarts // bm).astype(I32), (padded // bm).astype(I32), nblk_used, xs,
                  moe_w_gate, moe_w_up, moe_w_down)
    return _combine(dest, wts, base, mod, seq, final_g, ob)


def kernel(x, c, w_ada, b_ada, norm1_g, w_in, rw_mu, rw_w0, rw_w2, rw_a0, rw_a2, rw_g2, rw_k_k, rw_k_a, rw_r_k, rw_lnx_g, rw_lnx_b, gm_ln_g, gm_ln_b, gm_w_s, gm_b_s, w_br_rwkv, w_br_gmlp, w_out, norm2_g, router_w, router_b, moe_w_gate, moe_w_up, moe_w_down, sh_w_gate, sh_w_up, sh_w_down, final_g):
    bsz, seq, d = x.shape
    assert d == D_MODEL and w_ada.shape[0] == 1, "single-layer block of width D_MODEL"
    assert seq % IN_TM == 0 and seq % RW_BLOCK == 0 and seq % MG_TM == 0 and IN_TM % GM_CHUNK == 0
    assert bsz % RW_NB == 0 and (bsz * seq) % DSP_TM == 0 and seq % CMB_TM == 0
    out = _layer(x.reshape(bsz * seq, d), bsz, seq, c, w_ada[0], b_ada[0], norm1_g[0], w_in[0],
                 rw_mu[0], rw_w0[0], rw_w2[0], rw_a0[0], rw_a2[0], rw_g2[0], rw_k_k[0], rw_k_a[0],
                 rw_r_k[0].reshape(-1), rw_lnx_g[0], rw_lnx_b[0], gm_ln_g[0].reshape(-1),
                 gm_ln_b[0].reshape(-1), gm_w_s[0], gm_b_s[0], w_br_rwkv[0], w_br_gmlp[0],
                 w_out[0], norm2_g[0], router_w[0], router_b[0], moe_w_gate[0], moe_w_up[0],
                 moe_w_down[0], sh_w_gate[0], sh_w_up[0], sh_w_down[0], final_g)
    return out.reshape(bsz, seq, d)
```

```python
import functools

import jax
import jax.numpy as jnp
from jax import lax
from jax.experimental import pallas as pl
from jax.experimental.pallas import tpu as pltpu

F32 = jnp.float32
BF16 = jnp.bfloat16
I32 = jnp.int32

D_MODEL = 1024
RW_WIDTH = 512
RW_HEADS = 8
RW_HEAD_DIM = 64
RW_DECAY_LORA = 64
RW_AAA_LORA = 64
RW_GATE_LORA = 128
RW_COLS = 3 * RW_WIDTH + RW_DECAY_LORA + RW_AAA_LORA + RW_GATE_LORA
RW_LNX_EPS = 64e-5
GM_WIDTH = 512
GM_GROUP_DIM = 64
GM_GROUPS = 8
GM_CHUNK = 128
N_EXPERTS = 256
TOP_K = 8
EXPERT_DIM = 256
SHARED_DIM = 256
ROUTED_SCALE = 2.5
NORM_EPS = 1e-6
LN_EPS = 1e-5

LANES = 128
SUBLANES = 8
ROW_TILE = D_MODEL // LANES
VMEM_LIMIT = 56 * 1024 * 1024

WKV_CHUNK = 64
WKV_QUAD = 4
QW = WKV_QUAD * RW_HEAD_DIM
RW_BLOCK = 256
RW_NB = 4
IN_TM = 512
MG_TM = 512
EXP_BM = 256
EXP_IN_SLOTS = 4
DSP_TM = 2048
CMB_TM = 128
CMB_SLOTS = 3


def _dot(a, b):
    return jnp.dot(a, b, preferred_element_type=F32)


def _dot_nt(a, b):
    return lax.dot_general(a, b, (((1,), (1,)), ((), ())), preferred_element_type=F32)


def _dot_tn(a, b):
    return lax.dot_general(a, b, (((0,), (0,)), ((), ())), preferred_element_type=F32)


def _split2(a):
    hi = a.astype(BF16)
    lo = (a - hi.astype(F32)).astype(BF16)
    return hi, lo


def _split3(a):
    h1 = a.astype(BF16)
    r1 = a - h1.astype(F32)
    h2 = r1.astype(BF16)
    h3 = (r1 - h2.astype(F32)).astype(BF16)
    return h1, h2, h3


def _dot_hp(a, b):
    ah, al = _split2(a)
    bh, bl = _split2(b)
    return _dot(ah, bh) + _dot(al, bh) + _dot(ah, bl)


def _dot_exact_rhs(a, b_bf16):
    h1, h2, h3 = _split3(a)
    return _dot(h1, b_bf16) + _dot(h2, b_bf16) + _dot(h3, b_bf16)


def _sigmoid(x):
    return 1.0 / (1.0 + jnp.exp(-x))


def _rmsnorm_rows(x, g):
    return x * lax.rsqrt(jnp.mean(x * x, axis=-1, keepdims=True) + NORM_EPS) * g


def _to_row_tiles(ref, x):
    m = x.shape[0]
    for c in range(ROW_TILE):
        ref[pl.ds(c, m, stride=ROW_TILE), :] = x[:, c * LANES:(c + 1) * LANES]


def _from_row_tiles(ref):
    m = ref.shape[0] // ROW_TILE
    return jnp.concatenate([ref[pl.ds(c, m, stride=ROW_TILE), :] for c in range(ROW_TILE)], axis=1)


def _params(sem):
    return pltpu.CompilerParams(dimension_semantics=sem, vmem_limit_bytes=VMEM_LIMIT)


def _ada_kernel(c_ref, w_ref, b_ref, o_ref):
    c = c_ref[...]
    s = c * _sigmoid(c)
    o_ref[...] = _dot_hp(s, w_ref[...]) + b_ref[...]


def _ada(c, w, b):
    bsz, d = c.shape
    n = w.shape[1]
    tn = 1024
    return pl.pallas_call(
        _ada_kernel,
        out_shape=jax.ShapeDtypeStruct((bsz, n), F32),
        grid=(n // tn,),
        in_specs=[pl.BlockSpec((bsz, d), lambda j: (0, 0)),
                  pl.BlockSpec((d, tn), lambda j: (0, j)),
                  pl.BlockSpec((1, tn), lambda j: (0, j))],
        out_specs=pl.BlockSpec((bsz, tn), lambda j: (0, j)),
        compiler_params=_params(("arbitrary",)),
    )(c, w, b.reshape(1, n))


def _inproj_kernel(x_ref, mod_ref, g_ref, w_ref, lg_ref, lb_ref, ws_ref, bst_ref, e_ref,
                   prw_ref, ogm_ref, gt_ref):
    mod = mod_ref[0]
    h = _rmsnorm_rows(x_ref[...], g_ref[...]) * (1.0 + mod[1:2]) + mod[0:1]
    hb = h.astype(BF16)
    prw_ref[...] = _dot(hb, w_ref[:, :RW_COLS])
    pgm = _dot(hb, w_ref[:, RW_COLS:RW_COLS + 2 * GM_WIDTH])
    _gmlp_tile(jax.nn.gelu(pgm, approximate=True), lg_ref, lb_ref, ws_ref, bst_ref, e_ref, ogm_ref)
    pgt = _dot(hb, w_ref[:, RW_COLS + 2 * GM_WIDTH:])
    gt_ref[...] = _sigmoid(pgt).astype(BF16)


def _inproj(x2, mod, g1, w_in_bf, seq, ln_g, ln_b, w_s, b_s):
    n, d = x2.shape
    tm = IN_TM
    per_b = seq // tm
    cols = w_in_bf.shape[1]
    w = GM_WIDTH
    gid = jnp.arange(w, dtype=I32) // GM_GROUP_DIM
    e = (gid[:, None] == gid[None, :]).astype(BF16)
    const = lambda shape: pl.BlockSpec(shape, lambda i: (0,) * len(shape))
    return pl.pallas_call(
        _inproj_kernel,
        out_shape=(jax.ShapeDtypeStruct((n, RW_COLS), F32),
                   jax.ShapeDtypeStruct((n, w), BF16),
                   jax.ShapeDtypeStruct((n, 2 * d), BF16)),
        grid=(n // tm,),
        in_specs=[pl.BlockSpec((tm, d), lambda i: (i, 0)),
                  pl.BlockSpec((1, 6, d), lambda i: (i // per_b, 0, 0)),
                  const((1, d)),
                  pl.BlockSpec((d, cols), lambda i: (0, 0), pipeline_mode=pl.Buffered(1)),
                  const((1, w)), const((1, w)), const((GM_GROUPS, GM_CHUNK, GM_CHUNK)),
                  const((GM_CHUNK, GM_GROUPS)), const((w, w))],
        out_specs=(pl.BlockSpec((tm, RW_COLS), lambda i: (i, 0)),
                   pl.BlockSpec((tm, w), lambda i: (i, 0)),
                   pl.BlockSpec((tm, 2 * d), lambda i: (i, 0))),
        compiler_params=_params(("arbitrary",)),
    )(x2, mod, g1.reshape(1, d), w_in_bf, ln_g.reshape(1, w), ln_b.reshape(1, w), w_s, b_s.T, e)


def _rwkv_kernel(p_ref, mu_ref, w0_ref, w2_ref, a0_ref, a2_ref, g2_ref, kk_ref, ka_ref,
                 rk_ref, lg_ref, lb_ref, e_ref, o_ref, prev_ref, s_ref):
    nb = RW_NB
    tb = RW_BLOCK
    t = WKV_CHUNK
    c = RW_WIDTH
    nq = RW_HEADS // WKV_QUAD
    nch = tb // t

    @pl.when(pl.program_id(1) == 0)
    def _():
        prev_ref[...] = jnp.zeros_like(prev_ref)
        s_ref[...] = jnp.zeros_like(s_ref)

    p = p_ref[...].reshape(nb * tb, RW_COLS)
    row = lax.broadcasted_iota(I32, p.shape, 0)
    prev = pltpu.roll(p, 1, 0)
    for bi in range(nb):
        prev = jnp.where(row == bi * tb, prev_ref[bi:bi + 1, :], prev)
        prev_ref[bi:bi + 1, :] = p[(bi + 1) * tb - 1:(bi + 1) * tb, :]
    xs = p + (prev - p) * mu_ref[...]
    r = xs[:, 0:c]
    k = xs[:, c:2 * c]
    v = xs[:, 2 * c:3 * c]
    xwa = xs[:, 3 * c:3 * c + LANES]
    xg = xs[:, 3 * c + LANES:3 * c + 2 * LANES]

    wl = w0_ref[...] + _dot_hp(jnp.tanh(xwa), w2_ref[...])
    nz = -wl
    w_log = -(jnp.maximum(nz, 0.0) + jnp.log(1.0 + jnp.exp(-jnp.abs(nz)))) - 0.5
    lw = -jnp.exp(w_log)
    a_sig = _sigmoid(a0_ref[...] + _dot_hp(xwa, a2_ref[...]))
    g = _dot(_sigmoid(xg).astype(BF16), g2_ref[...].astype(BF16))

    e = e_ref[...]

    def headsum(z):
        hi, lo = _split2(z)
        return _dot(hi, e) + _dot(lo, e)

    kk = k * kk_ref[...]
    kk = kk * lax.rsqrt(jnp.maximum(headsum(kk * kk), 1e-24))
    k2 = k * (1.0 + (a_sig - 1.0) * ka_ref[...])
    a_v = -kk
    b_v = kk * a_sig

    ri = lax.broadcasted_iota(I32, (tb, tb), 0)
    ci = lax.broadcasted_iota(I32, (tb, tb), 1)
    tri = jnp.where((ri // t == ci // t) & (ri >= ci), 1.0, 0.0).astype(BF16)
    cum = jnp.concatenate([_dot_exact_lhs(tri, lw[bi * tb:(bi + 1) * tb, :]) for bi in range(nb)], axis=0)
    ends = [cum[(j + 1) * t - 1:(j + 1) * t, :] for j in range(nb * nch)]
    w_tots = [jnp.exp(z) for z in ends]
    w_tot_rows = jnp.concatenate([jnp.broadcast_to(z, (t, c)) for z in w_tots], axis=0)

    w_abs = jnp.exp(cum)
    at = (a_v * jnp.exp(cum - lw)).astype(BF16)
    rt = (r * w_abs).astype(BF16)
    w_inv = jnp.exp(-cum)
    kh = (k2 * w_inv).astype(BF16)
    bh = (b_v * w_inv).astype(BF16)
    vb = v.astype(BF16)
    w_rem = w_tot_rows * w_inv
    kw = (k2 * w_rem).astype(BF16)
    bw = (b_v * w_rem).astype(BF16)

    lane = lax.broadcasted_iota(I32, (1, QW), 1)
    hmask = [jnp.where(lane // RW_HEAD_DIM == h, 1.0, 0.0).astype(BF16) for h in range(WKV_QUAD)]

    def stack_heads(z):
        return jnp.concatenate([z * m for m in hmask], axis=0)

    tt = lax.broadcasted_iota(I32, (t, QW), 0)
    ss = lax.broadcasted_iota(I32, (t, QW), 1) % t
    m_strict = tt > ss
    m_incl = tt >= ss
    eye = jnp.where(tt == ss, 1.0, 0.0)
    r4 = lax.broadcasted_iota(I32, (QW, QW), 0)
    c4 = lax.broadcasted_iota(I32, (QW, QW), 1)
    bdm = jnp.where((r4 // t) == (c4 // t), 1.0, 0.0)

    keys = [(bi, ch, q) for bi in range(nb) for ch in range(nch) for q in range(nq)]
    sl = {(bi, ch, q): (slice(bi * tb + ch * t, bi * tb + (ch + 1) * t), slice(q * QW, (q + 1) * QW))
          for bi, ch, q in keys}
    ar = {key: jnp.concatenate([at[sl[key]], rt[sl[key]]], axis=0) for key in keys}
    aa = {key: _dot_nt(ar[key], jnp.concatenate([stack_heads(kh[sl[key]]), stack_heads(bh[sl[key]])],
                                                 axis=0))
          for key in keys}
    a_ak = {key: jnp.where(m_strict, aa[key][0:t, 0:QW], 0.0).astype(BF16) for key in keys}
    a_ab = {key: jnp.where(m_strict, aa[key][0:t, QW:2 * QW], 0.0) for key in keys}
    a_rk = {key: jnp.where(m_incl, aa[key][t:2 * t, 0:QW], 0.0).astype(BF16) for key in keys}
    a_rb = {key: jnp.where(m_incl, aa[key][t:2 * t, QW:2 * QW], 0.0).astype(BF16) for key in keys}
    tinv = {key: eye + a_ab[key] for key in keys}
    pw = {}
    for key in keys:
        lb = a_ab[key].astype(BF16)
        pw[key] = _dot(lb, stack_heads(lb)).astype(BF16)
    for _ in range(4):
        both = {key: _dot(jnp.concatenate([pw[key], tinv[key].astype(BF16)], axis=0), stack_heads(pw[key]))
                for key in keys}
        tinv = {key: tinv[key] + both[key][t:2 * t] for key in keys}
        pw = {key: both[key][0:t].astype(BF16) for key in keys}
    tinv = {key: (tinv[key] + _dot(tinv[key].astype(BF16), stack_heads(pw[key]))).astype(BF16)
            for key in keys}
    avv = {key: _dot(jnp.concatenate([a_ak[key], a_rk[key]], axis=0), stack_heads(vb[sl[key]]))
           for key in keys}

    y_blk = {}
    for ch in range(nch):
        ks = [(bi, ch, q) for bi in range(nb) for q in range(nq)]
        s_q = {key: s_ref[key[0] * nq + key[2]] for key in ks}
        sst = {key: _dot_nt(ar[key], s_q[key].astype(BF16)) for key in ks}
        ub = {key: _dot(tinv[key], stack_heads((sst[key][0:t] + avv[key][0:t]).astype(BF16))).astype(BF16)
              for key in ks}
        upd = {key: _dot_tn(jnp.concatenate([vb[sl[key]], ub[key]], axis=0),
                            jnp.concatenate([kw[sl[key]], bw[sl[key]]], axis=0)) for key in ks}
        for key in ks:
            rs, cs = sl[key]
            w_tot = w_tots[key[0] * nch + ch][:, cs]
            s_ref[key[0] * nq + key[2]] = s_q[key] * w_tot + upd[key] * bdm
        for key in ks:
            y_blk[key] = sst[key][t:2 * t] + avv[key][t:2 * t] + _dot(a_rb[key], stack_heads(ub[key]))
    y = jnp.concatenate([jnp.concatenate([y_blk[bi, ch, q] for q in range(nq)], axis=1)
                         for bi in range(nb) for ch in range(nch)], axis=0)

    inv_n = 1.0 / RW_HEAD_DIM
    m = headsum(y) * inv_n
    dlt = y - m
    var = headsum(dlt * dlt) * inv_n
    yn = dlt * lax.rsqrt(var + RW_LNX_EPS) * lg_ref[...] + lb_ref[...]
    bonus = headsum(r * k2 * rk_ref[...]) * v
    o_ref[...] = ((yn + bonus) * g).astype(BF16).reshape(nb, tb, c)


def _dot_exact_lhs(tri_bf16, a):
    h1, h2, h3 = _split3(a)
    return _dot(tri_bf16, h1) + _dot(tri_bf16, h2) + _dot(tri_bf16, h3)


def _rwkv(p_rw, bsz, seq, mu, w0, w2, a0, a2, g2, k_k, k_a, r_k, lnx_g, lnx_b):
    c = RW_WIDTH
    tb = RW_BLOCK
    nb = RW_NB
    zeros = jnp.zeros((RW_DECAY_LORA, c), F32)
    w2p = jnp.concatenate([w2, zeros], axis=0)
    a2p = jnp.concatenate([zeros, a2], axis=0)
    hid = jnp.arange(c, dtype=I32) // RW_HEAD_DIM
    e = (hid[:, None] == hid[None, :]).astype(BF16)
    row = lambda z: z.reshape(1, -1)
    const = lambda shape: pl.BlockSpec(shape, lambda b, j: (0,) * len(shape))
    out = pl.pallas_call(
        _rwkv_kernel,
        out_shape=jax.ShapeDtypeStruct((bsz, seq, c), BF16),
        grid=(bsz // nb, seq // tb),
        in_specs=[pl.BlockSpec((nb, tb, RW_COLS), lambda b, j: (b, j, 0)),
                  const((1, RW_COLS)), const((1, c)), const((LANES, c)), const((1, c)),
                  const((LANES, c)), const((RW_GATE_LORA, c)), const((1, c)), const((1, c)),
                  const((1, c)), const((1, c)), const((1, c)), const((c, c))],
        out_specs=pl.BlockSpec((nb, tb, c), lambda b, j: (b, j, 0)),
        scratch_shapes=[pltpu.VMEM((nb, RW_COLS), F32),
                        pltpu.VMEM((nb * (RW_HEADS // WKV_QUAD), QW, QW), F32)],
        compiler_params=_params(("arbitrary", "arbitrary")),
    )(p_rw.reshape(bsz, seq, RW_COLS), row(mu), row(w0), w2p, row(a0), a2p, g2, row(k_k), row(k_a),
      row(r_k), row(lnx_g), row(lnx_b), e)
    return out.reshape(bsz * seq, c)


def _gmlp_tile(z, lg_ref, lb_ref, ws_ref, bst_ref, e_ref, o_ref):
    w = GM_WIDTH
    ch = GM_CHUNK
    u = z[:, :w]
    v = z[:, w:]
    e = e_ref[...]
    inv_n = 1.0 / GM_GROUP_DIM

    def groupsum(x):
        hi, lo = _split2(x)
        return _dot(hi, e) + _dot(lo, e)

    m = groupsum(v) * inv_n
    dlt = v - m
    var = groupsum(dlt * dlt) * inv_n
    vn = (dlt * lax.rsqrt(var + LN_EPS) * lg_ref[...] + lb_ref[...]).astype(BF16)

    ri = lax.broadcasted_iota(I32, (ch, ch), 0)
    ci = lax.broadcasted_iota(I32, (ch, ch), 1)
    low = ri >= ci
    lane = lax.broadcasted_iota(I32, (1, LANES), 1)
    m_lo = jnp.where(lane < GM_GROUP_DIM, 1.0, 0.0).astype(BF16)
    m_hi = jnp.where(lane >= GM_GROUP_DIM, 1.0, 0.0).astype(BF16)
    bst = bst_ref[...]
    for pr in range(GM_GROUPS // 2):
        g0, g1 = 2 * pr, 2 * pr + 1
        wcat = jnp.concatenate([jnp.where(low, ws_ref[g0], 0.0), jnp.where(low, ws_ref[g1], 0.0)],
                               axis=1).astype(BF16)
        bias = jnp.where(lane < GM_GROUP_DIM, bst[:, g0:g0 + 1], bst[:, g1:g1 + 1])
        ls = slice(pr * LANES, (pr + 1) * LANES)
        for cc in range(z.shape[0] // ch):
            rs = slice(cc * ch, (cc + 1) * ch)
            vp = vn[rs, ls]
            rhs = jnp.concatenate([vp * m_lo, vp * m_hi], axis=0)
            sv = _dot(wcat, rhs) + bias
            o_ref[rs, ls] = (u[rs, ls] * sv).astype(BF16)


def _merge_kernel(x_ref, orw_ref, ogm_ref, gt_ref, mod_ref, wr_ref, wg_ref, wo_ref, g2_ref,
                  rw_ref, rb_ref, sg_ref, su_ref, sd_ref,
                  base_ref, h2_ref, idx_ref, pos_ref, wts_ref, cnt_ref):
    d = D_MODEL
    tm = MG_TM
    ne = N_EXPERTS

    @pl.when(pl.program_id(0) == 0)
    def _():
        cnt_ref[...] = jnp.zeros_like(cnt_ref)

    mod = mod_ref[0]
    gt = gt_ref[...]
    br = _dot(orw_ref[...], wr_ref[...])
    bg = _dot(ogm_ref[...], wg_ref[...])
    merged = gt[:, :d].astype(F32) * br + gt[:, d:].astype(F32) * bg
    x1 = x_ref[...] + mod[2:3] * _dot(merged.astype(BF16), wo_ref[...])
    h2 = _rmsnorm_rows(x1, g2_ref[...]) * (1.0 + mod[4:5]) + mod[3:4]

    _to_row_tiles(h2_ref, h2)
    logits = _dot_hp(h2, rw_ref[...])

    hb = h2.astype(BF16)
    sgate = _dot(hb, sg_ref[...])
    act = (sgate * _sigmoid(sgate) * _dot(hb, su_ref[...])).astype(BF16)
    base_ref[...] = x1 + mod[5:6] * _dot(act, sd_ref[...])

    scores = _sigmoid(logits)
    cur = scores + rb_ref[...]
    lane_e = lax.broadcasted_iota(I32, (tm, ne), 1).astype(F32)
    ri = lax.broadcasted_iota(I32, (tm, tm), 0)
    ci = lax.broadcasted_iota(I32, (tm, tm), 1)
    tri = jnp.where(ri > ci, 1.0, 0.0).astype(BF16)
    onehots, sels, ixs = [], [], []
    for _ in range(TOP_K):
        mx = jnp.max(cur, axis=-1, keepdims=True)
        ix = jnp.min(jnp.where(cur == mx, lane_e, float(ne)), axis=-1, keepdims=True)
        oh = lane_e == ix
        sels.append(jnp.sum(jnp.where(oh, scores, 0.0), axis=-1, keepdims=True))
        cur = jnp.where(oh, -jnp.inf, cur)
        onehots.append(oh)
        ixs.append(ix)
    chosen = onehots[0]
    for oh in onehots[1:]:
        chosen = chosen | oh
    cmask = jnp.where(chosen, 1.0, 0.0)
    denom = sels[0]
    for s in sels[1:]:
        denom = denom + s
    scale = ROUTED_SCALE / denom
    rank = cnt_ref[0:1, :] + _dot(tri, cmask.astype(BF16))
    cnt_ref[0:1, :] = cnt_ref[0:1, :] + jnp.sum(cmask, axis=0, keepdims=True)
    lane_o = lax.broadcasted_iota(I32, (tm, LANES), 1)
    idx_o = jnp.zeros((tm, LANES), F32)
    pos_o = jnp.zeros((tm, LANES), F32)
    wts_o = jnp.zeros((tm, LANES), F32)
    for kslot in range(TOP_K):
        pos_k = jnp.sum(jnp.where(onehots[kslot], rank, 0.0), axis=-1, keepdims=True)
        here = lane_o == kslot
        idx_o = jnp.where(here, ixs[kslot], idx_o)
        pos_o = jnp.where(here, pos_k, pos_o)
        wts_o = jnp.where(here, sels[kslot] * scale, wts_o)
    idx_ref[...] = idx_o.astype(I32)
    pos_ref[...] = pos_o.astype(I32)
    wts_ref[...] = wts_o


def _merge(x2, o_rw, o_gm, gt, mod, seq, wr, wg, wo, g2n, router_w, router_b, sg, su, sd):
    n, d = x2.shape
    tm = MG_TM
    per_b = seq // tm
    ne = N_EXPERTS
    tile = lambda w: pl.BlockSpec((tm, w), lambda i: (i, 0))
    const2 = lambda a, b: pl.BlockSpec((a, b), lambda i: (0, 0))
    return pl.pallas_call(
        _merge_kernel,
        out_shape=(jax.ShapeDtypeStruct((n, d), F32),
                   jax.ShapeDtypeStruct((n * ROW_TILE, LANES), F32),
                   jax.ShapeDtypeStruct((n, LANES), I32),
                   jax.ShapeDtypeStruct((n, LANES), I32),
                   jax.ShapeDtypeStruct((n, LANES), F32),
                   jax.ShapeDtypeStruct((8, ne), F32)),
        grid=(n // tm,),
        in_specs=[tile(d), tile(RW_WIDTH), tile(GM_WIDTH), tile(2 * d),
                  pl.BlockSpec((1, 6, d), lambda i: (i // per_b, 0, 0)),
                  const2(RW_WIDTH, d), const2(GM_WIDTH, d), const2(d, d), const2(1, d),
                  const2(d, ne), const2(1, ne),
                  const2(d, SHARED_DIM), const2(d, SHARED_DIM), const2(SHARED_DIM, d)],
        out_specs=(tile(d), pl.BlockSpec((tm * ROW_TILE, LANES), lambda i: (i, 0)),
                   tile(LANES), tile(LANES), tile(LANES),
                   pl.BlockSpec((8, ne), lambda i: (0, 0))),
        compiler_params=_params(("arbitrary",)),
    )(x2, o_rw, o_gm, gt, mod, wr, wg, wo, g2n.reshape(1, d), router_w, router_b.reshape(1, ne),
      sg, su, sd)


def _destmap_kernel(idx_ref, pos_ref, ps_ref, o_ref):
    tm = idx_ref.shape[0]
    idx = idx_ref[...]
    ps = ps_ref[...]
    lane_e = lax.broadcasted_iota(I32, (tm, N_EXPERTS), 1)
    lane_o = lax.broadcasted_iota(I32, (tm, LANES), 1)
    start = jnp.zeros((tm, LANES), F32)
    for kslot in range(TOP_K):
        hit = lane_e == idx[:, kslot:kslot + 1]
        s_k = jnp.sum(jnp.where(hit, ps, 0.0), axis=-1, keepdims=True)
        start = jnp.where(lane_o == kslot, s_k, start)
    o_ref[...] = pos_ref[...] + start.astype(I32)


def _destmap(idx, pos, pstarts_f32):
    n = idx.shape[0]
    tm = 1024
    return pl.pallas_call(
        _destmap_kernel,
        out_shape=jax.ShapeDtypeStruct((n, LANES), I32),
        grid=(n // tm,),
        in_specs=[pl.BlockSpec((tm, LANES), lambda i: (i, 0)),
                  pl.BlockSpec((tm, LANES), lambda i: (i, 0)),
                  pl.BlockSpec((1, N_EXPERTS), lambda i: (0, 0))],
        out_specs=pl.BlockSpec((tm, LANES), lambda i: (i, 0)),
        compiler_params=_params(("arbitrary",)),
    )(idx, pos, pstarts_f32.reshape(1, N_EXPERTS))


def _dispatch_kernel(zf_ref, dest_ref, h_ref, xs_ref, zbuf, sem):
    tm = DSP_TM
    bm = EXP_BM

    @pl.when(pl.program_id(0) == 0)
    def _():
        zbuf[...] = jnp.zeros_like(zbuf)

        def zero_copy(e):
            first = pl.multiple_of(zf_ref[e] * ROW_TILE, bm * ROW_TILE)
            return pltpu.make_async_copy(zbuf, xs_ref.at[pl.ds(first, bm * ROW_TILE)], sem)

        def zstart(e, carry):
            @pl.when(zf_ref[e] >= 0)
            def _():
                zero_copy(e).start()
            return carry

        def zwait(e, carry):
            @pl.when(zf_ref[e] >= 0)
            def _():
                zero_copy(e).wait()
            return carry

        lax.fori_loop(0, 2 * N_EXPERTS, zstart, 0)
        lax.fori_loop(0, 2 * N_EXPERTS, zwait, 0)

    def issue(grp, carry):
        tok0 = pl.multiple_of(grp * SUBLANES, SUBLANES)
        for u in range(SUBLANES):
            for kslot in range(TOP_K):
                dst = pl.multiple_of(dest_ref[(tok0 + u) * TOP_K + kslot] * ROW_TILE, ROW_TILE)
                pltpu.make_async_copy(h_ref.at[pl.ds((tok0 + u) * ROW_TILE, ROW_TILE)],
                                      xs_ref.at[pl.ds(dst, ROW_TILE)],
                                      sem).start(priority=kslot % 2)
        return carry

    lax.fori_loop(0, tm // SUBLANES, issue, 0)
    for kslot in range(TOP_K):
        pltpu.make_async_copy(h_ref, xs_ref.at[pl.ds(0, tm * ROW_TILE)], sem).wait()


def _dispatch(zfill, dest_flat, h2f, p_rows):
    n = h2f.shape[0] // ROW_TILE
    tm = DSP_TM
    grid_spec = pltpu.PrefetchScalarGridSpec(
        num_scalar_prefetch=1,
        grid=(n // tm,),
        in_specs=[pl.BlockSpec((tm * TOP_K,), lambda i, zf: (i,), memory_space=pltpu.SMEM),
                  pl.BlockSpec((tm * ROW_TILE, LANES), lambda i, zf: (i, 0))],
        out_specs=pl.BlockSpec(memory_space=pl.ANY),
        scratch_shapes=[pltpu.VMEM((EXP_BM * ROW_TILE, LANES), F32), pltpu.SemaphoreType.DMA(())],
    )
    return pl.pallas_call(
        _dispatch_kernel,
        out_shape=jax.ShapeDtypeStruct((p_rows * ROW_TILE, LANES), F32),
        grid_spec=grid_spec,
        compiler_params=_params(("arbitrary",)),
    )(zfill, dest_flat, h2f)


def _experts_kernel(bstart_ref, bcnt_ref, nb_ref, wg_ref, wu_ref, wd_ref, xs_ref, o_ref,
                    xbuf, obuf, act_scr, wgb, wub, wdb, in_sems, out_sems):
    e = pl.program_id(0)
    bm = EXP_BM
    nb = nb_ref[0]
    g0 = bstart_ref[e]
    nblk = o_ref.shape[0] // (bm * ROW_TILE)

    def rows(g):
        return pl.ds(pl.multiple_of(g * (bm * ROW_TILE), bm * ROW_TILE), bm * ROW_TILE)

    def in_copy(g, slot):
        return pltpu.make_async_copy(xs_ref.at[rows(g)], xbuf.at[slot], in_sems.at[slot])

    def out_copy(g, slot):
        return pltpu.make_async_copy(obuf.at[slot], o_ref.at[rows(g)], out_sems.at[slot])

    ahead = EXP_IN_SLOTS - 1
    for g_first in range(ahead):
        @pl.when((e == 0) & (nb > g_first))
        def _():
            in_copy(g_first, g_first).start()

    def fetch_rows(g):
        @pl.when(g + ahead < nb)
        def _():
            in_copy(g + ahead, lax.rem(g + ahead, EXP_IN_SLOTS)).start()

        in_copy(g, lax.rem(g, EXP_IN_SLOTS)).wait()

    def free_out_slot(g):
        @pl.when(g >= 2)
        def _():
            out_copy(g - 2, lax.rem(g, 2)).wait()

    def stage_a(g):
        xb = _from_row_tiles(xbuf.at[lax.rem(g, EXP_IN_SLOTS)]).astype(BF16)
        gate = _dot(xb, wgb[...])
        up = _dot(xb, wub[...])
        return (gate * _sigmoid(gate) * up).astype(BF16)

    def stage_b(g):
        return _dot(act_scr[lax.rem(g, 2)], wdb[...])

    def write_back(g, res):
        slot = lax.rem(g, 2)
        _to_row_tiles(obuf.at[slot], res)
        out_copy(g, slot).start()

    cnt = bcnt_ref[e]

    @pl.when(cnt > 0)
    def _():
        wgb[...] = wg_ref[0].astype(BF16)
        wub[...] = wu_ref[0].astype(BF16)
        wdb[...] = wd_ref[0].astype(BF16)
        fetch_rows(g0)
        act_scr[lax.rem(g0, 2)] = stage_a(g0)

        def body(j, carry):
            g = g0 + j
            fetch_rows(g)
            free_out_slot(g - 1)
            res = stage_b(g - 1)
            act_scr[lax.rem(g, 2)] = stage_a(g)
            write_back(g - 1, res)
            return carry

        lax.fori_loop(1, cnt, body, 0)
        last = g0 + cnt - 1
        free_out_slot(last)
        write_back(last, stage_b(last))

    @pl.when(e == pl.num_programs(0) - 1)
    def _():
        @pl.when(nb >= 2)
        def _():
            out_copy(nb - 2, lax.rem(nb, 2)).wait()

        @pl.when(nb >= 1)
        def _():
            out_copy(nb - 1, lax.rem(nb + 1, 2)).wait()

        obuf[0] = jnp.zeros(obuf.shape[1:], obuf.dtype)

        def zstart(g, carry):
            out_copy(g, 0).start()
            return carry

        def zwait(g, carry):
            out_copy(g, 0).wait()
            return carry

        lax.fori_loop(nb, nblk, zstart, 0)
        lax.fori_loop(nb, nblk, zwait, 0)


def _experts(bstart, bcnt, nblk_used, xs, w_gate, w_up, w_down):
    d = D_MODEL
    bm = EXP_BM
    wspec = lambda shape: pl.BlockSpec(shape, lambda e, bs, bc, nb: (e, 0, 0))
    grid_spec = pltpu.PrefetchScalarGridSpec(
        num_scalar_prefetch=3,
        grid=(N_EXPERTS,),
        in_specs=[wspec((1, d, EXPERT_DIM)), wspec((1, d, EXPERT_DIM)), wspec((1, EXPERT_DIM, d)),
                  pl.BlockSpec(memory_space=pl.ANY)],
        out_specs=pl.BlockSpec(memory_space=pl.ANY),
        scratch_shapes=[pltpu.VMEM((EXP_IN_SLOTS, bm * ROW_TILE, LANES), F32),
                        pltpu.VMEM((2, bm * ROW_TILE, LANES), F32),
                        pltpu.VMEM((2, bm, EXPERT_DIM), BF16),
                        pltpu.VMEM((d, EXPERT_DIM), BF16),
                        pltpu.VMEM((d, EXPERT_DIM), BF16),
                        pltpu.VMEM((EXPERT_DIM, d), BF16),
                        pltpu.SemaphoreType.DMA((EXP_IN_SLOTS,)),
                        pltpu.SemaphoreType.DMA((2,))],
    )
    return pl.pallas_call(
        _experts_kernel,
        out_shape=jax.ShapeDtypeStruct(xs.shape, F32),
        grid_spec=grid_spec,
        compiler_params=_params(("arbitrary",)),
    )(bstart, bcnt, nblk_used, w_gate, w_up, w_down, xs)


def _combine_kernel(dcur_ref, dnx1_ref, dnx2_ref, wts_ref, base_ref, mod_ref, fg_ref, ob_ref, o_ref,
                    buf, sems):
    tm = CMB_TM
    nt = pl.num_programs(0)
    i = pl.program_id(0)
    slot = lax.rem(i, CMB_SLOTS)
    gate2 = mod_ref[0][5:6]
    fg = fg_ref[...]

    def issue_group(d_ref, s, tok0):
        for u in range(SUBLANES):
            for kslot in range(TOP_K):
                src = pl.multiple_of(d_ref[(tok0 + u) * TOP_K + kslot] * ROW_TILE, ROW_TILE)
                pltpu.make_async_copy(ob_ref.at[pl.ds(src, ROW_TILE)],
                                      buf.at[s, kslot, pl.ds((tok0 + u) * ROW_TILE, ROW_TILE)],
                                      sems.at[s]).start(priority=kslot % 2)

    def issue_tile(d_ref, s):
        def body(grp, carry):
            issue_group(d_ref, s, pl.multiple_of(grp * SUBLANES, SUBLANES))
            return carry
        lax.fori_loop(0, tm // SUBLANES, body, 0)

    def reduce_group(tok0):
        rows = pl.ds(tok0, SUBLANES)
        w8 = wts_ref[rows, :]
        wk = [w8[:, kslot:kslot + 1] for kslot in range(TOP_K)]
        x2 = []
        for c in range(ROW_TILE):
            ls = slice(c * LANES, (c + 1) * LANES)
            acc = None
            for kslot in range(TOP_K):
                part = wk[kslot] * buf[slot, kslot, pl.ds(tok0 * ROW_TILE + c, SUBLANES, stride=ROW_TILE), :]
                acc = part if acc is None else acc + part
            x2.append(base_ref[rows, ls] + gate2[:, ls] * acc)
        ssq = x2[0] * x2[0]
        for c in range(1, ROW_TILE):
            ssq = ssq + x2[c] * x2[c]
        inv = lax.rsqrt(jnp.sum(ssq, axis=-1, keepdims=True) * (1.0 / D_MODEL) + NORM_EPS)
        for c in range(ROW_TILE):
            ls = slice(c * LANES, (c + 1) * LANES)
            o_ref[rows, ls] = x2[c] * inv * fg[:, ls]

    @pl.when(i == 0)
    def _():
        issue_tile(dcur_ref, 0)

        @pl.when(nt > 1)
        def _():
            issue_tile(dnx1_ref, 1)

    for kslot in range(TOP_K):
        pltpu.make_async_copy(ob_ref.at[pl.ds(0, tm * ROW_TILE)], buf.at[slot, kslot], sems.at[slot]).wait()

    for s in range(CMB_SLOTS):
        @pl.when((i + 2 < nt) & (slot == (s + 1) % CMB_SLOTS))
        def _():
            def body(grp, carry):
                tok0 = pl.multiple_of(grp * SUBLANES, SUBLANES)
                reduce_group(tok0)
                issue_group(dnx2_ref, s, tok0)
                return carry
            lax.fori_loop(0, tm // SUBLANES, body, 0)

    @pl.when(i + 2 >= nt)
    def _():
        def body(grp, carry):
            reduce_group(pl.multiple_of(grp * SUBLANES, SUBLANES))
            return carry
        lax.fori_loop(0, tm // SUBLANES, body, 0)


def _combine(dest_flat, wts, base, mod, seq, final_g, ob):
    n, d = base.shape
    tm = CMB_TM
    per_b = seq // tm
    nt = n // tm
    dest_tile = lambda k: pl.BlockSpec((tm * TOP_K,), lambda i: (jnp.minimum(i + k, nt - 1),),
                                       memory_space=pltpu.SMEM)
    return pl.pallas_call(
        _combine_kernel,
        out_shape=jax.ShapeDtypeStruct((n, d), F32),
        grid=(nt,),
        in_specs=[dest_tile(0), dest_tile(1), dest_tile(2),
                  pl.BlockSpec((tm, LANES), lambda i: (i, 0)),
                  pl.BlockSpec((tm, d), lambda i: (i, 0)),
                  pl.BlockSpec((1, 6, d), lambda i: (i // per_b, 0, 0)),
                  pl.BlockSpec((1, d), lambda i: (0, 0)),
                  pl.BlockSpec(memory_space=pl.ANY)],
        out_specs=pl.BlockSpec((tm, d), lambda i: (i, 0)),
        scratch_shapes=[pltpu.VMEM((CMB_SLOTS, TOP_K, tm * ROW_TILE, LANES), F32),
                        pltpu.SemaphoreType.DMA((CMB_SLOTS,))],
        compiler_params=_params(("arbitrary",)),
    )(dest_flat, dest_flat, dest_flat, wts, base, mod, final_g.reshape(1, d), ob)


def _layer(x2, bsz, seq, c, w_ada, b_ada, norm1_g, w_in, rw_mu, rw_w0, rw_w2, rw_a0, rw_a2, rw_g2,
           rw_k_k, rw_k_a, rw_r_k, rw_lnx_g, rw_lnx_b, gm_ln_g, gm_ln_b, gm_w_s, gm_b_s,
           w_br_rwkv, w_br_gmlp, w_out, norm2_g, router_w, router_b, moe_w_gate, moe_w_up,
           moe_w_down, sh_w_gate, sh_w_up, sh_w_down, final_g):
    n, d = x2.shape
    mod = _ada(c, w_ada, b_ada).reshape(bsz, 6, d)
    p_rw, o_gm, gt = _inproj(x2, mod, norm1_g, w_in.astype(BF16), seq, gm_ln_g, gm_ln_b, gm_w_s, gm_b_s)
    o_rw = _rwkv(p_rw, bsz, seq, rw_mu, rw_w0, rw_w2, rw_a0, rw_a2, rw_g2, rw_k_k, rw_k_a,
                 rw_r_k, rw_lnx_g, rw_lnx_b)
    base, h2f, idx, pos, wts, cnt = _merge(
        x2, o_rw, o_gm, gt, mod, seq, w_br_rwkv.astype(BF16), w_br_gmlp.astype(BF16),
        w_out.astype(BF16), norm2_g, router_w, router_b, sh_w_gate.astype(BF16),
        sh_w_up.astype(BF16), sh_w_down.astype(BF16))

    bm = EXP_BM
    counts = cnt[0].astype(I32)
    padded = (counts + bm - 1) // bm * bm
    pends = jnp.cumsum(padded)
    pstarts = pends - padded
    p_rows = n * TOP_K + N_EXPERTS * bm
    nblk = p_rows // bm
    nblk_used = (pends[-1:] // bm).astype(I32)
    tail = nblk_used[0] + jnp.arange(N_EXPERTS, dtype=I32)
    zfill = jnp.concatenate([jnp.where(padded > 0, pends - bm, -1),
                             jnp.where(tail < nblk, tail * bm, -1)]).astype(I32)

    dest = _destmap(idx, pos, pstarts.astype(F32))[:, :TOP_K].reshape(-1)
    xs = _dispatch(zfill, dest, h2f, p_rows)
    ob = _experts((pstarts // bm).astype(I32), (padded // bm).astype(I32), nblk_used, xs,
                  moe_w_gate, moe_w_up, moe_w_down)
    return _combine(dest, wts, base, mod, seq, final_g, ob)


def kernel(x, c, w_ada, b_ada, norm1_g, w_in, rw_mu, rw_w0, rw_w2, rw_a0, rw_a2, rw_g2, rw_k_k, rw_k_a, rw_r_k, rw_lnx_g, rw_lnx_b, gm_ln_g, gm_ln_b, gm_w_s, gm_b_s, w_br_rwkv, w_br_gmlp, w_out, norm2_g, router_w, router_b, moe_w_gate, moe_w_up, moe_w_down, sh_w_gate, sh_w_up, sh_w_down, final_g):
    bsz, seq, d = x.shape
    assert d == D_MODEL and w_ada.shape[0] == 1, "single-layer block of width D_MODEL"
    assert seq % IN_TM == 0 and seq % RW_BLOCK == 0 and seq % MG_TM == 0 and IN_TM % GM_CHUNK == 0
    assert bsz % RW_NB == 0 and (bsz * seq) % DSP_TM == 0 and seq % CMB_TM == 0
    out = _layer(x.reshape(bsz * seq, d), bsz, seq, c, w_ada[0], b_ada[0], norm1_g[0], w_in[0],
                 rw_mu[0], rw_w0[0], rw_w2[0], rw_a0[0], rw_a2[0], rw_g2[0], rw_k_k[0], rw_k_a[0],
                 rw_r_k[0].reshape(-1), rw_lnx_g[0], rw_lnx_b[0], gm_ln_g[0].reshape(-1),
                 gm_ln_b[0].reshape(-1), gm_w_s[0], gm_b_s[0], w_br_rwkv[0], w_br_gmlp[0],
                 w_out[0], norm2_g[0], router_w[0], router_b[0], moe_w_gate[0], moe_w_up[0],
                 moe_w_down[0], sh_w_gate[0], sh_w_up[0], sh_w_down[0], final_g)
    return out.reshape(bsz, seq, d)
```
